```python
import math
import jax
import jax.numpy as jnp
from jax import lax
import numpy as np

D_MODEL = 2048
BATCH = 8
SEQ = 4096
DEPTH = 2

N_META = 16
N_BRANCH = 3
EPS = 1e-6
ROPE_THETA = 10000.0

DA_HEADS = 4
DA_V_DIM = D_MODEL // 2 // DA_HEADS
DA_HEAD_DIM = DA_V_DIM // 2
DA_WIDTH = DA_HEADS * DA_V_DIM
DA_QK_WIDTH = DA_HEADS * 2 * DA_HEAD_DIM
Q_BLOCK = 128

CONV_CH = D_MODEL // 2
CONV_WIDTH = 31

GLA_HEADS = 4
GLA_VW = D_MODEL // 2
GLA_KW = D_MODEL // 4
GLA_DV = GLA_VW // GLA_HEADS
GLA_DK = GLA_KW // GLA_HEADS
GLA_GATE_RANK = 16
GLA_TAU = 16.0
GLA_CHUNK = 64

D_FF = 4 * D_MODEL

IN_SIZES = (DA_QK_WIDTH, DA_QK_WIDTH, DA_WIDTH, 2 * CONV_CH, GLA_KW, GLA_KW, GLA_VW, 2 * GLA_GATE_RANK, GLA_VW, N_BRANCH * D_MODEL)
IN_WIDTH = sum(IN_SIZES)

kernel_name = 'hybrid_gated_diffattn_conformer_gla_encoder'


def rmsnorm(x, g):
    xf = x.astype(jnp.float32)
    y = xf * lax.rsqrt(jnp.mean(xf * xf, axis=-1, keepdims=True) + EPS)
    return (y * g.astype(jnp.float32)).astype(x.dtype)


def layernorm(x, g, b):
    xf = x.astype(jnp.float32)
    mu = jnp.mean(xf, axis=-1, keepdims=True)
    var = jnp.mean(jnp.square(xf - mu), axis=-1, keepdims=True)
    y = (xf - mu) * lax.rsqrt(var + EPS)
    return (y * g.astype(jnp.float32) + b.astype(jnp.float32)).astype(x.dtype)


def rope_tables(T, dim):
    inv_freq = 1.0 / (ROPE_THETA ** (jnp.arange(0, dim, 2, dtype=jnp.float32) / dim))
    ang = jnp.arange(T, dtype=jnp.float32)[:, None] * inv_freq[None, :]
    return jnp.cos(ang), jnp.sin(ang)


def apply_rope(x, cos, sin):
    half = x.shape[-1] // 2
    x1 = x[..., :half].astype(jnp.float32)
    x2 = x[..., half:].astype(jnp.float32)
    return jnp.concatenate([x1 * cos - x2 * sin, x2 * cos + x1 * sin], axis=-1).astype(x.dtype)


def diff_attention(q, k, v, lam, lam_init, subln_g):
    B, T, _ = q.shape
    H, d = DA_HEADS, DA_HEAD_DIM
    q = q.reshape(B, T, H, 2, d).transpose(0, 2, 3, 1, 4)
    k = k.reshape(B, T, H, 2, d).transpose(0, 2, 3, 1, 4)
    v = v.reshape(B, T, H, DA_V_DIM).transpose(0, 2, 1, 3)
    cos, sin = rope_tables(T, d)
    q = apply_rope(q, cos, sin) * (d ** -0.5)
    k = apply_rope(k, cos, sin)
    n_blk = -(-T // Q_BLOCK)
    pad = n_blk * Q_BLOCK - T
    qp = jnp.pad(q, ((0, 0), (0, 0), (0, 0), (0, pad), (0, 0)))
    qb = qp.reshape(B, H, 2, n_blk, Q_BLOCK, d).transpose(3, 0, 1, 2, 4, 5)

    def block(q_blk):
        s = jnp.einsum('bhmqd,bhmkd->bhmqk', q_blk, k).astype(jnp.float32)
        p = jax.nn.softmax(s, axis=-1)
        a = p[:, :, 0] - lam * p[:, :, 1]
        return jnp.einsum('bhqk,bhkv->bhqv', a.astype(v.dtype), v)

    o = lax.map(block, qb)
    o = o.transpose(1, 2, 0, 3, 4).reshape(B, H, n_blk * Q_BLOCK, DA_V_DIM)[:, :, :T]
    o = rmsnorm(o, subln_g) * (1.0 - lam_init)
    return o.transpose(0, 2, 1, 3).reshape(B, T, DA_WIDTH)


def conv_module(u, dw_w, dw_b, ln_g, ln_b):
    a, gate = jnp.split(u, 2, axis=-1)
    z = a * jax.nn.sigmoid(gate)
    half = CONV_WIDTH // 2
    z = lax.conv_general_dilated(z, dw_w.astype(z.dtype)[:, None, :], window_strides=(1,), padding=[(half, half)], dimension_numbers=('NWC', 'WIO', 'NWC'), feature_group_count=CONV_CH) + dw_b.astype(z.dtype)
    z = layernorm(z, ln_g, ln_b)
    return jax.nn.silu(z)


def gla_scan(q, k, v, log_a):
    B, H, L, dk = q.shape
    dv = v.shape[-1]
    n = L // GLA_CHUNK

    def to_chunks(t):
        return t.reshape(B, H, n, GLA_CHUNK, t.shape[-1]).transpose(2, 0, 1, 3, 4)

    mask = jnp.tril(jnp.ones((GLA_CHUNK, GLA_CHUNK), dtype=bool))

    def step(S, inp):
        qc, kc, vc, gc = inp
        b = jnp.cumsum(gc, axis=2)
        o_inter = jnp.einsum('bhcd,bhdv->bhcv', qc * jnp.exp(b), S)
        rel = jnp.where(mask[:, :, None], b[:, :, :, None, :] - b[:, :, None, :, :], -jnp.inf)
        A = jnp.einsum('bhid,bhjd,bhijd->bhij', qc, kc, jnp.exp(rel))
        o_intra = jnp.einsum('bhij,bhjv->bhiv', A, vc)
        b_last = b[:, :, -1, :]
        S = S * jnp.exp(b_last)[..., None] + jnp.einsum('bhcd,bhcv->bhdv', kc * jnp.exp(b_last[:, :, None, :] - b), vc)
        return S, o_inter + o_intra

    S0 = jnp.zeros((B, H, dk, dv), jnp.float32)
    _, o = lax.scan(step, S0, (to_chunks(q), to_chunks(k), to_chunks(v), to_chunks(log_a)))
    return o.transpose(1, 2, 0, 3, 4).reshape(B, H, L, dv)


def gla_bidirectional(q, k, v, gate_lr, r, gw_f, gb_f, gw_b, gb_b, norm_g):
    B, T, _ = q.shape
    f32 = jnp.float32
    lr_f, lr_b = jnp.split(gate_lr.astype(f32), 2, axis=-1)
    la_f = jax.nn.log_sigmoid(lr_f @ gw_f.astype(f32) + gb_f.astype(f32)) / GLA_TAU
    la_b = jax.nn.log_sigmoid(lr_b @ gw_b.astype(f32) + gb_b.astype(f32)) / GLA_TAU

    def heads(t, d):
        return t.astype(f32).reshape(B, T, GLA_HEADS, d).transpose(0, 2, 1, 3)

    lead = (GLA_CHUNK - N_META % GLA_CHUNK) % GLA_CHUNK

    def pad(t):
        return jnp.pad(t, ((0, 0), (0, 0), (lead, 0), (0, 0)))

    qh = pad(heads(q, GLA_DK) * (GLA_DK ** -0.5))
    kh = pad(heads(k, GLA_DK))
    vh = pad(heads(v, GLA_DV))
    af = pad(heads(la_f, GLA_DK))
    ab = pad(heads(la_b, GLA_DK))
    o_f = gla_scan(qh, kh, vh, af)
    o_b = jnp.flip(gla_scan(jnp.flip(qh, 2), jnp.flip(kh, 2), jnp.flip(vh, 2), jnp.flip(ab, 2)), 2)
    o = (o_f + o_b)[:, :, lead:]
    o = rmsnorm(o, norm_g).transpose(0, 2, 1, 3).reshape(B, T, GLA_VW)
    return o.astype(r.dtype) * jax.nn.silu(r)


def setup_inputs(seed: int = 0) -> dict:
    key = jax.random.key(seed)
    ks = jax.random.split(key, 24)
    f32 = jnp.float32
    L, D = DEPTH, D_MODEL

    def nrm(k, shape, scale):
        return jax.random.normal(k, shape, f32) * scale

    return {
        'x': nrm(ks[0], (BATCH, SEQ, D), 1.0),
        'meta_tokens': nrm(ks[1], (N_META, D), 1.0),
        'mix_norm_g': 1.0 + nrm(ks[2], (L, D), 0.02),
        'w_in': nrm(ks[3], (L, D, IN_WIDTH), D ** -0.5),
        'da_lambda': nrm(ks[4], (L, 4, DA_HEAD_DIM), 0.1),
        'da_subln_g': 1.0 + nrm(ks[5], (L, DA_V_DIM), 0.02),
        'w_da_proj': nrm(ks[6], (L, DA_WIDTH, D), DA_WIDTH ** -0.5),
        'conv_dw_w': nrm(ks[7], (L, CONV_WIDTH, CONV_CH), CONV_WIDTH ** -0.5),
        'conv_dw_b': nrm(ks[8], (L, CONV_CH), 0.02),
        'conv_ln_g': 1.0 + nrm(ks[9], (L, CONV_CH), 0.02),
        'conv_ln_b': nrm(ks[10], (L, CONV_CH), 0.02),
        'w_conv_proj': nrm(ks[11], (L, CONV_CH, D), CONV_CH ** -0.5),
        'b_conv_proj': nrm(ks[12], (L, D), 0.02),
        'gla_gate_w_fwd': nrm(ks[13], (L, GLA_GATE_RANK, GLA_KW), GLA_GATE_RANK ** -0.5),
        'gla_gate_b_fwd': nrm(ks[14], (L, GLA_KW), 0.1),
        'gla_gate_w_bwd': nrm(ks[15], (L, GLA_GATE_RANK, GLA_KW), GLA_GATE_RANK ** -0.5),
        'gla_gate_b_bwd': nrm(ks[16], (L, GLA_KW), 0.1),
        'gla_norm_g': 1.0 + nrm(ks[17], (L, GLA_DV), 0.02),
        'w_gla_proj': nrm(ks[18], (L, GLA_VW, D), GLA_VW ** -0.5),
        'w_out': nrm(ks[19], (L, D, D), D ** -0.5),
        'mlp_norm_g': 1.0 + nrm(ks[20], (L, D), 0.02),
        'w_mlp_in': nrm(ks[21], (L, D, D_FF), D ** -0.5),
        'w_mlp_out': nrm(ks[22], (L, D_FF, D), D_FF ** -0.5),
        'final_norm_g': 1.0 + nrm(ks[23], (D,), 0.02),
    }


def reference(x, meta_tokens, mix_norm_g, w_in, da_lambda, da_subln_g, w_da_proj, conv_dw_w, conv_dw_b, conv_ln_g, conv_ln_b, w_conv_proj, b_conv_proj, gla_gate_w_fwd, gla_gate_b_fwd, gla_gate_w_bwd, gla_gate_b_bwd, gla_norm_g, w_gla_proj, w_out, mlp_norm_g, w_mlp_in, w_mlp_out, final_norm_g):
    B = x.shape[0]
    h = jnp.concatenate([jnp.broadcast_to(meta_tokens[None].astype(x.dtype), (B, N_META, D_MODEL)), x], axis=1)
    T = h.shape[1]
    split_at = [int(i) for i in np.cumsum(IN_SIZES)[:-1]]
    for l in range(DEPTH):
        lam_init = 0.8 - 0.6 * math.exp(-0.3 * l)
        a = rmsnorm(h, mix_norm_g[l])
        u = a @ w_in[l]
        dq, dk, dv, cu, gq, gk, gv, glr, gr, gates = jnp.split(u, split_at, axis=-1)
        lq1, lk1, lq2, lk2 = da_lambda[l].astype(jnp.float32)
        lam = jnp.exp(jnp.sum(lq1 * lk1)) - jnp.exp(jnp.sum(lq2 * lk2)) + lam_init
        y_da = diff_attention(dq, dk, dv, lam, lam_init, da_subln_g[l]) @ w_da_proj[l]
        y_cv = conv_module(cu, conv_dw_w[l], conv_dw_b[l], conv_ln_g[l], conv_ln_b[l]) @ w_conv_proj[l] + b_conv_proj[l]
        y_gla = gla_bidirectional(gq, gk, gv, glr, gr, gla_gate_w_fwd[l], gla_gate_b_fwd[l], gla_gate_w_bwd[l], gla_gate_b_bwd[l], gla_norm_g[l]) @ w_gla_proj[l]
        g = jax.nn.sigmoid(gates.reshape(B, T, N_BRANCH, D_MODEL))
        merged = g[:, :, 0] * y_da + g[:, :, 1] * y_cv + g[:, :, 2] * y_gla
        h = h + merged @ w_out[l]
        a = rmsnorm(h, mlp_norm_g[l])
        h = h + jnp.square(jax.nn.relu(a @ w_mlp_in[l])) @ w_mlp_out[l]
    return rmsnorm(h, final_norm_g)[:, N_META:]
```

```python
import functools
import math

import jax
import jax.numpy as jnp
from jax import lax
from jax.experimental import pallas as pl
from jax.experimental.pallas import tpu as pltpu

F32 = jnp.float32
BF16 = jnp.bfloat16

EPS = 1e-6
ROPE_THETA = 10000.0
N_META = 16
N_BRANCH = 3
DA_HEADS = 4
GLA_HEADS = 4
GLA_GATE_RANK = 16
GLA_TAU = 16.0
CONV_WIDTH = 31

LANES = 128
PAD_FRONT = LANES - N_META
GLA_BLOCK = 128
GLA_SUB = 64
CONV_HALO = 16
VMEM_LIMIT = 56 * 1024 * 1024


def _cparams(*sem):
    return pltpu.CompilerParams(dimension_semantics=sem, vmem_limit_bytes=VMEM_LIMIT)


def _row_tile(t_pad):
    for n in range(1, t_pad // 16 + 1):
        if t_pad % n == 0 and (t_pad // n) % 16 == 0 and t_pad // n <= 1152:
            return t_pad // n
    raise ValueError(f"no row tile for padded sequence {t_pad}")


def _largest_divisor(n, candidates):
    for c in candidates:
        if n % c == 0:
            return c
    raise ValueError(f"no tile among {candidates} divides {n}")


def _prenorm_kernel(h_ref, g_ref, o_ref):
    x = h_ref[...]
    ms = jnp.mean(x * x, axis=-1, keepdims=True)
    o_ref[...] = (x * lax.rsqrt(ms + EPS) * g_ref[...]).astype(o_ref.dtype)


def _prenorm(h, g, tm):
    m, d = h.shape
    return pl.pallas_call(
        _prenorm_kernel,
        grid=(m // tm,),
        in_specs=[pl.BlockSpec((tm, d), lambda i: (i, 0)),
                  pl.BlockSpec((1, d), lambda i: (0, 0))],
        out_specs=pl.BlockSpec((tm, d), lambda i: (i, 0)),
        out_shape=jax.ShapeDtypeStruct((m, d), BF16),
        compiler_params=_cparams("parallel"),
        name="prenorm",
    )(h, g.reshape(1, d))


def _proj_kernel(a_ref, w_ref, o_ref):
    o_ref[...] = jnp.dot(a_ref[...], w_ref[...], preferred_element_type=F32).astype(o_ref.dtype)


def _proj(a, w, tm, tn, out_dtype, name):
    m, k = a.shape
    n = w.shape[1]
    return pl.pallas_call(
        _proj_kernel,
        grid=(m // tm, n // tn),
        in_specs=[pl.BlockSpec((tm, k), lambda i, j: (i, 0)),
                  pl.BlockSpec((k, tn), lambda i, j: (0, j))],
        out_specs=pl.BlockSpec((tm, tn), lambda i, j: (i, j)),
        out_shape=jax.ShapeDtypeStruct((m, n), out_dtype),
        compiler_params=_cparams("parallel", "arbitrary"),
        name=name,
    )(a, w)


def _rope_proj_kernel(a_ref, w_ref, cos_ref, sin_ref, o_ref, *, q_scale):
    acc = jnp.dot(a_ref[...], w_ref[...], preferred_element_type=F32)
    scale = jnp.where(pl.program_id(1) == 0, q_scale, 1.0).astype(F32)
    cos = cos_ref[...]
    sin = sin_ref[...]
    for g in range(acc.shape[1] // LANES):
        x = acc[:, g * LANES:(g + 1) * LANES]
        swapped = pltpu.roll(x, LANES // 2, axis=1)
        o_ref[:, g * LANES:(g + 1) * LANES] = ((x * cos + swapped * sin) * scale).astype(o_ref.dtype)


def _rope_proj(a, w, cos, sin, tm, tiles_per_batch, q_scale):
    m, k = a.shape
    n = w.shape[1]
    tn = n // 2
    return pl.pallas_call(
        functools.partial(_rope_proj_kernel, q_scale=q_scale),
        grid=(m // tm, 2),
        in_specs=[pl.BlockSpec((tm, k), lambda i, j: (i, 0)),
                  pl.BlockSpec((k, tn), lambda i, j: (0, j)),
                  pl.BlockSpec((tm, LANES), lambda i, j: (i % tiles_per_batch, 0)),
                  pl.BlockSpec((tm, LANES), lambda i, j: (i % tiles_per_batch, 0))],
        out_specs=pl.BlockSpec((tm, tn), lambda i, j: (i, j)),
        out_shape=jax.ShapeDtypeStruct((m, n), BF16),
        compiler_params=_cparams("parallel", "arbitrary"),
        name="rope_proj",
    )(a, w, cos, sin)


def _attn_kernel(lam_ref, kbias_ref, q_ref, k_ref, v_ref, g_ref, o_ref, *, lam_init):
    d = q_ref.shape[1] // 2
    lam4 = lam_ref[...]
    dot1 = jnp.sum(lam4[0:1] * lam4[1:2], axis=-1, keepdims=True)
    dot2 = jnp.sum(lam4[2:3] * lam4[3:4], axis=-1, keepdims=True)
    lam = jnp.exp(dot1) - jnp.exp(dot2) + lam_init
    kbias = kbias_ref[...]
    es, rs = [], []
    for m in range(2):
        q = q_ref[:, m * d:(m + 1) * d]
        k = k_ref[:, m * d:(m + 1) * d]
        s = lax.dot_general(q, k, (((1,), (1,)), ((), ())), preferred_element_type=F32) + kbias
        e = jnp.exp(s - jnp.max(s, axis=-1, keepdims=True))
        es.append(e)
        rs.append(1.0 / jnp.sum(e, axis=-1, keepdims=True))
    a = es[0] * rs[0] - es[1] * (lam * rs[1])
    o = jnp.dot(a.astype(BF16), v_ref[...], preferred_element_type=F32)
    o = o * lax.rsqrt(jnp.mean(o * o, axis=-1, keepdims=True) + EPS) * g_ref[...]
    o_ref[...] = (o * (1.0 - lam_init)).astype(o_ref.dtype)


def _diff_attention(qk, ubf, lam4, kbias, subln_g, batch, t_pad, tq, lam_init):
    m = qk.shape[0]
    hw = 2 * lam4.shape[1]
    nq = t_pad // tq
    return pl.pallas_call(
        functools.partial(_attn_kernel, lam_init=lam_init),
        grid=(batch, DA_HEADS, nq),
        in_specs=[pl.BlockSpec(lam4.shape, lambda b, h, i: (0, 0)),
                  pl.BlockSpec((1, t_pad), lambda b, h, i: (0, 0)),
                  pl.BlockSpec((tq, hw), lambda b, h, i: (b * nq + i, h)),
                  pl.BlockSpec((t_pad, hw), lambda b, h, i: (b, DA_HEADS + h)),
                  pl.BlockSpec((t_pad, hw), lambda b, h, i: (b, h)),
                  pl.BlockSpec((1, hw), lambda b, h, i: (0, 0))],
        out_specs=pl.BlockSpec((tq, hw), lambda b, h, i: (b * nq + i, h)),
        out_shape=jax.ShapeDtypeStruct((m, DA_HEADS * hw), BF16),
        compiler_params=_cparams("parallel", "parallel", "arbitrary"),
        name="diff_attention",
    )(lam4, kbias, qk, qk, ubf, subln_g.reshape(1, hw))


def _conv_kernel(prev_ref, cur_ref, next_ref, w_ref, b_ref, lg_ref, lb_ref, o_ref, z_scr, *, n_tiles):
    i = pl.program_id(1)
    c = o_ref.shape[1]
    tt = o_ref.shape[0]

    def glu(u):
        return u[:, :c] * jax.nn.sigmoid(u[:, c:])

    z_scr[0:CONV_HALO, :] = jnp.where(i > 0, glu(prev_ref[...]), 0.0)
    z_scr[CONV_HALO:CONV_HALO + tt, :] = glu(cur_ref[...])
    z_scr[CONV_HALO + tt:, :] = jnp.where(i < n_tiles - 1, glu(next_ref[...]), 0.0)
    half = CONV_WIDTH // 2
    w = w_ref[...]
    acc = jnp.zeros((tt, c), F32)
    for k in range(CONV_WIDTH):
        start = CONV_HALO - half + k
        acc = acc + z_scr[start:start + tt, :] * w[k:k + 1, :]
    y = acc + b_ref[...]
    mu = jnp.mean(y, axis=-1, keepdims=True)
    yc = y - mu
    var = jnp.mean(yc * yc, axis=-1, keepdims=True)
    y = yc * lax.rsqrt(var + EPS) * lg_ref[...] + lb_ref[...]
    o_ref[...] = (y * jax.nn.sigmoid(y)).astype(o_ref.dtype)


def _conv_module(u32, dw_w, dw_b, ln_g, ln_b, batch, t_pad, tt):
    m = u32.shape[0]
    c = dw_w.shape[1]
    nt = t_pad // tt
    hb = tt // CONV_HALO
    last_halo = m // CONV_HALO - 1
    return pl.pallas_call(
        functools.partial(_conv_kernel, n_tiles=nt),
        grid=(batch, nt),
        in_specs=[pl.BlockSpec((CONV_HALO, 2 * c), lambda b, i: (jnp.maximum((b * nt + i) * hb - 1, 0), 0)),
                  pl.BlockSpec((tt, 2 * c), lambda b, i: (b * nt + i, 0)),
                  pl.BlockSpec((CONV_HALO, 2 * c), lambda b, i: (jnp.minimum((b * nt + i + 1) * hb, last_halo), 0)),
                  pl.BlockSpec((CONV_WIDTH, c), lambda b, i: (0, 0)),
                  pl.BlockSpec((1, c), lambda b, i: (0, 0)),
                  pl.BlockSpec((1, c), lambda b, i: (0, 0)),
                  pl.BlockSpec((1, c), lambda b, i: (0, 0))],
        out_specs=pl.BlockSpec((tt, c), lambda b, i: (b * nt + i, 0)),
        out_shape=jax.ShapeDtypeStruct((m, c), BF16),
        scratch_shapes=[pltpu.VMEM((tt + 2 * CONV_HALO, c), F32)],
        compiler_params=_cparams("parallel", "arbitrary"),
        name="conv_module",
    )(u32, u32, u32, dw_w, dw_b.reshape(1, c), ln_g.reshape(1, c), ln_b.reshape(1, c))


def _log_sigmoid(x):
    return jnp.minimum(x, 0.0) - jnp.log(1.0 + jnp.exp(-jnp.abs(x)))


def _gla_direction(q_ref, k_ref, v_ref, lr_ref, wg_ref, bg_ref, o_ref, s_scr, *, reverse, q_scale):
    rows_total = q_ref.shape[0]
    heads = s_scr.shape[0]
    dv, dk = s_scr.shape[1], s_scr.shape[2]
    ri = lax.broadcasted_iota(jnp.int32, (GLA_SUB, GLA_SUB), 0)
    ci = lax.broadcasted_iota(jnp.int32, (GLA_SUB, GLA_SUB), 1)
    keep = (ci >= ri) if reverse else (ci <= ri)
    tri = jnp.where(keep, 1.0, 0.0).astype(BF16)
    edge = 0 if reverse else GLA_SUB - 1
    gate = jnp.dot(lr_ref[...], wg_ref[...], preferred_element_type=F32) + bg_ref[...]
    log_a = _log_sigmoid(gate) * (1.0 / GLA_TAU)
    subs = range(rows_total // GLA_SUB)
    for s in (reversed(subs) if reverse else subs):
        r0 = s * GLA_SUB
        la = log_a[r0:r0 + GLA_SUB, :]
        la_hi = la.astype(BF16)
        la_lo = (la - la_hi.astype(F32)).astype(BF16)
        b = (jnp.dot(tri, la_hi, preferred_element_type=F32)
             + jnp.dot(tri, la_lo, preferred_element_type=F32))
        b_edge = b[edge:edge + 1, :]
        eq = jnp.exp(b)
        ek = jnp.exp(-b)
        es = jnp.exp(b_edge - b)
        ed = jnp.exp(b_edge)
        for h in range(heads):
            cs = slice(h * dk, (h + 1) * dk)
            vs = slice(h * dv, (h + 1) * dv)
            q = q_ref[r0:r0 + GLA_SUB, cs] * q_scale
            k = k_ref[r0:r0 + GLA_SUB, cs]
            v = v_ref[r0:r0 + GLA_SUB, vs]
            qt = (q * eq[:, cs]).astype(BF16)
            kt = (k * ek[:, cs]).astype(BF16)
            ks = (k * es[:, cs]).astype(BF16)
            st = s_scr[h]
            o_inter = lax.dot_general(qt, st.astype(BF16), (((1,), (1,)), ((), ())),
                                      preferred_element_type=F32)
            a = lax.dot_general(qt, kt, (((1,), (1,)), ((), ())), preferred_element_type=F32)
            a = jnp.where(keep, a, 0.0).astype(BF16)
            o_ref[r0:r0 + GLA_SUB, vs] = o_inter + jnp.dot(a, v, preferred_element_type=F32)
            ds = lax.dot_general(v, ks, (((0,), (0,)), ((), ())), preferred_element_type=F32)
            s_scr[h] = st * ed[:, cs] + ds


def _gla_kernel(qf, kf, vf, lrf, qb, kb, vb, lrb, wgf, bgf, wgb, bgb, of_ref, ob_ref, sf_scr, sb_scr,
                *, q_scale):
    @pl.when(pl.program_id(1) == 0)
    def _():
        sf_scr[...] = jnp.zeros_like(sf_scr)
        sb_scr[...] = jnp.zeros_like(sb_scr)

    _gla_direction(qf, kf, vf, lrf, wgf, bgf, of_ref, sf_scr, reverse=False, q_scale=q_scale)
    _gla_direction(qb, kb, vb, lrb, wgb, bgb, ob_ref, sb_scr, reverse=True, q_scale=q_scale)


def _gla_scans(u32, ubf, wg_f, bg_f, wg_b, bg_b, batch, t_pad, kw, vw, col_q, col_k, col_v, col_lr):
    m = u32.shape[0]
    nc = t_pad // GLA_BLOCK
    dk = kw // GLA_HEADS
    dv = vw // GLA_HEADS

    def fwd(col):
        return lambda b, c: (b * nc + c, col)

    def bwd(col):
        return lambda b, c: (b * nc + nc - 1 - c, col)

    def direction_specs(idx):
        return [pl.BlockSpec((GLA_BLOCK, kw), idx(col_q)),
                pl.BlockSpec((GLA_BLOCK, kw), idx(col_k)),
                pl.BlockSpec((GLA_BLOCK, vw), idx(col_v)),
                pl.BlockSpec((GLA_BLOCK, LANES), idx(col_lr))]

    const = lambda b, c: (0, 0)
    gate_specs = [pl.BlockSpec((LANES, kw), const), pl.BlockSpec((1, kw), const),
                  pl.BlockSpec((LANES, kw), const), pl.BlockSpec((1, kw), const)]
    return pl.pallas_call(
        functools.partial(_gla_kernel, q_scale=dk ** -0.5),
        grid=(batch, nc),
        in_specs=direction_specs(fwd) + direction_specs(bwd) + gate_specs,
        out_specs=[pl.BlockSpec((GLA_BLOCK, vw), fwd(0)), pl.BlockSpec((GLA_BLOCK, vw), bwd(0))],
        out_shape=[jax.ShapeDtypeStruct((m, vw), F32), jax.ShapeDtypeStruct((m, vw), F32)],
        scratch_shapes=[pltpu.VMEM((GLA_HEADS, dv, dk), F32), pltpu.VMEM((GLA_HEADS, dv, dk), F32)],
        compiler_params=_cparams("parallel", "arbitrary"),
        name="gla_scans",
    )(u32, u32, ubf, ubf, u32, u32, ubf, ubf, wg_f, bg_f.reshape(1, kw), wg_b, bg_b.reshape(1, kw))


def _gla_post_kernel(of_ref, ob_ref, r_ref, g_ref, y_ref):
    dv = g_ref.shape[1]
    for h in range(y_ref.shape[1] // dv):
        vs = slice(h * dv, (h + 1) * dv)
        o = of_ref[:, vs] + ob_ref[:, vs]
        o = o * lax.rsqrt(jnp.mean(o * o, axis=-1, keepdims=True) + EPS) * g_ref[...]
        r = r_ref[:, vs]
        y_ref[:, vs] = (o * (r * jax.nn.sigmoid(r))).astype(y_ref.dtype)


def _gla_post(o_f, o_b, u32, norm_g, tm, col_r):
    m, vw = o_f.shape
    dv = norm_g.shape[0]
    return pl.pallas_call(
        _gla_post_kernel,
        grid=(m // tm,),
        in_specs=[pl.BlockSpec((tm, vw), lambda i: (i, 0)),
                  pl.BlockSpec((tm, vw), lambda i: (i, 0)),
                  pl.BlockSpec((tm, vw), lambda i: (i, col_r)),
                  pl.BlockSpec((1, dv), lambda i: (0, 0))],
        out_specs=pl.BlockSpec((tm, vw), lambda i: (i, 0)),
        out_shape=jax.ShapeDtypeStruct((m, vw), BF16),
        compiler_params=_cparams("parallel"),
        name="gla_post",
    )(o_f, o_b, u32, norm_g.reshape(1, dv))


def _merge_kernel(xa_ref, xc_ref, xg_ref, wa_ref, wc_ref, wg_ref, bc_ref, ga_ref, gc_ref, gg_ref, o_ref):
    ya = jnp.dot(xa_ref[...], wa_ref[...], preferred_element_type=F32)
    yc = jnp.dot(xc_ref[...], wc_ref[...], preferred_element_type=F32) + bc_ref[...]
    yg = jnp.dot(xg_ref[...], wg_ref[...], preferred_element_type=F32)
    merged = (jax.nn.sigmoid(ga_ref[...]) * ya + jax.nn.sigmoid(gc_ref[...]) * yc
              + jax.nn.sigmoid(gg_ref[...]) * yg)
    o_ref[...] = merged.astype(o_ref.dtype)


def _merge(x_da, x_cv, x_gla, w_da, w_cv, w_gla, b_cv, u32, tm, tn, gate_col0):
    m, kdim = x_da.shape
    d = w_da.shape[1]
    nj = d // tn
    x_spec = pl.BlockSpec((tm, kdim), lambda i, j: (i, 0))
    w_spec = pl.BlockSpec((kdim, tn), lambda i, j: (0, j))

    def gate_spec(branch):
        return pl.BlockSpec((tm, tn), lambda i, j: (i, gate_col0 // tn + branch * nj + j))

    return pl.pallas_call(
        _merge_kernel,
        grid=(m // tm, nj),
        in_specs=[x_spec, x_spec, x_spec, w_spec, w_spec, w_spec,
                  pl.BlockSpec((1, tn), lambda i, j: (0, j)),
                  gate_spec(0), gate_spec(1), gate_spec(2)],
        out_specs=pl.BlockSpec((tm, tn), lambda i, j: (i, j)),
        out_shape=jax.ShapeDtypeStruct((m, d), BF16),
        compiler_params=_cparams("parallel", "arbitrary"),
        name="branch_merge",
    )(x_da, x_cv, x_gla, w_da, w_cv, w_gla, b_cv.reshape(1, d), u32, u32, u32)


def _out_proj_kernel(x_ref, w_ref, h_ref, o_ref, *, tiles_per_batch):
    y = jnp.dot(x_ref[...], w_ref[...], preferred_element_type=F32)
    row = lax.broadcasted_iota(jnp.int32, y.shape, 0)
    first = pl.program_id(0) % tiles_per_batch == 0
    y = jnp.where(jnp.logical_and(first, row < PAD_FRONT), 0.0, y)
    o_ref[...] = h_ref[...] + y


def _out_proj(x, w, h, tm, tn, tiles_per_batch):
    m, kdim = x.shape
    d = w.shape[1]
    return pl.pallas_call(
        functools.partial(_out_proj_kernel, tiles_per_batch=tiles_per_batch),
        grid=(m // tm, d // tn),
        in_specs=[pl.BlockSpec((tm, kdim), lambda i, j: (i, 0)),
                  pl.BlockSpec((kdim, tn), lambda i, j: (0, j)),
                  pl.BlockSpec((tm, tn), lambda i, j: (i, j))],
        out_specs=pl.BlockSpec((tm, tn), lambda i, j: (i, j)),
        out_shape=jax.ShapeDtypeStruct((m, d), F32),
        compiler_params=_cparams("parallel", "arbitrary"),
        name="out_proj",
    )(x, w, h)


def _mlp_kernel(h_ref, g_ref, w1_ref, w2_ref, o_ref, a_scr):
    @pl.when(pl.program_id(1) == 0)
    def _():
        x = h_ref[...]
        ms = jnp.mean(x * x, axis=-1, keepdims=True)
        a_scr[...] = (x * lax.rsqrt(ms + EPS) * g_ref[...]).astype(a_scr.dtype)
        o_ref[...] = x

    t = jnp.maximum(jnp.dot(a_scr[...], w1_ref[...], preferred_element_type=F32), 0.0)
    o_ref[...] += jnp.dot((t * t).astype(BF16), w2_ref[...], preferred_element_type=F32)


def _mlp(h, g, w1, w2, tm, tf):
    m, d = h.shape
    ff = w1.shape[1]
    return pl.pallas_call(
        _mlp_kernel,
        grid=(m // tm, ff // tf),
        in_specs=[pl.BlockSpec((tm, d), lambda i, f: (i, 0)),
                  pl.BlockSpec((1, d), lambda i, f: (0, 0)),
                  pl.BlockSpec((d, tf), lambda i, f: (0, f)),
                  pl.BlockSpec((tf, d), lambda i, f: (f, 0))],
        out_specs=pl.BlockSpec((tm, d), lambda i, f: (i, 0)),
        out_shape=jax.ShapeDtypeStruct((m, d), F32),
        scratch_shapes=[pltpu.VMEM((tm, d), BF16)],
        compiler_params=_cparams("parallel", "arbitrary"),
        name="mlp",
    )(h, g.reshape(1, d), w1, w2)


def _final_norm_kernel(h_ref, g_ref, o_ref):
    x = h_ref[0]
    ms = jnp.mean(x * x, axis=-1, keepdims=True)
    o_ref[0] = x * lax.rsqrt(ms + EPS) * g_ref[...]


def _final_norm(h3, g, seq):
    batch, _, d = h3.shape
    tt = LANES
    return pl.pallas_call(
        _final_norm_kernel,
        grid=(batch, seq // tt),
        in_specs=[pl.BlockSpec((1, tt, d), lambda b, i: (b, i + 1, 0)),
                  pl.BlockSpec((1, d), lambda b, i: (0, 0))],
        out_specs=pl.BlockSpec((1, tt, d), lambda b, i: (b, i, 0)),
        out_shape=jax.ShapeDtypeStruct((batch, seq, d), F32),
        compiler_params=_cparams("parallel", "parallel"),
        name="final_norm",
    )(h3, g.reshape(1, d))


def kernel(x, meta_tokens, mix_norm_g, w_in, da_lambda, da_subln_g, w_da_proj, conv_dw_w, conv_dw_b, conv_ln_g, conv_ln_b, w_conv_proj, b_conv_proj, gla_gate_w_fwd, gla_gate_b_fwd, gla_gate_w_bwd, gla_gate_b_bwd, gla_norm_g, w_gla_proj, w_out, mlp_norm_g, w_mlp_in, w_mlp_out, final_norm_g):
    batch, seq, d = x.shape
    depth = w_in.shape[0]
    assert seq % LANES == 0 and meta_tokens.shape[0] == N_META
    t_pad = PAD_FRONT + N_META + seq
    m = batch * t_pad

    da_hd = da_lambda.shape[-1]
    da_w = w_da_proj.shape[1]
    da_qk = DA_HEADS * 2 * da_hd
    conv_c = conv_dw_w.shape[-1]
    gla_kw = gla_gate_w_fwd.shape[-1]
    gla_vw = w_gla_proj.shape[1]
    d_ff = w_mlp_in.shape[-1]
    assert da_hd == LANES and da_w == DA_HEADS * 2 * da_hd

    sizes = (da_qk, da_qk, da_w, 2 * conv_c, gla_kw, gla_kw, gla_vw, 2 * GLA_GATE_RANK, gla_vw, N_BRANCH * d)
    assert sum(sizes) == w_in.shape[-1]
    off = [0]
    for s in sizes:
        off.append(off[-1] + s)
    c_dq, c_dk, c_dv, c_cu, c_gq, c_gk, c_gv, c_lr, c_gr, c_gt = off[:-1]

    tm = _row_tile(t_pad)
    tiles_per_batch = t_pad // tm
    tq = _largest_divisor(t_pad, (384, 256, 128))
    tt_conv = LANES
    tn_f32 = 1024
    tn_merge = 512
    tf = 512

    pos = (jnp.arange(t_pad, dtype=F32) - PAD_FRONT)[:, None]
    inv_freq = 1.0 / (ROPE_THETA ** (jnp.arange(0, da_hd, 2, dtype=F32) / da_hd))
    ang = pos * inv_freq[None, :]
    cos_t = jnp.concatenate([jnp.cos(ang), jnp.cos(ang)], axis=-1)
    sin_t = jnp.concatenate([-jnp.sin(ang), jnp.sin(ang)], axis=-1)
    kbias = jnp.where(jnp.arange(t_pad) >= PAD_FRONT, 0.0, -1e30).astype(F32)[None, :]

    h = jnp.concatenate([jnp.zeros((batch, PAD_FRONT, d), F32),
                         jnp.broadcast_to(meta_tokens[None].astype(F32), (batch, N_META, d)),
                         x.astype(F32)], axis=1).reshape(m, d)

    for l in range(depth):
        lam_init = 0.8 - 0.6 * math.exp(-0.3 * l)
        wl = w_in[l]

        def cols(c0, width):
            return wl[:, c0:c0 + width]

        lr_pad = jnp.zeros((d, LANES - 2 * GLA_GATE_RANK), wl.dtype)
        w_rope = jnp.concatenate([cols(c_dq, da_qk), cols(c_dk, da_qk)], axis=1).astype(BF16)
        w_bf = jnp.concatenate([cols(c_dv, da_w), cols(c_gv, gla_vw), cols(c_lr, 2 * GLA_GATE_RANK), lr_pad],
                               axis=1).astype(BF16)
        w_f32 = jnp.concatenate([cols(c_cu, 2 * conv_c), cols(c_gq, gla_kw), cols(c_gk, gla_kw),
                                 cols(c_gr, gla_vw), cols(c_gt, N_BRANCH * d)], axis=1).astype(BF16)
        f_cu, f_gq, f_gk = 0, 2 * conv_c, 2 * conv_c + gla_kw
        f_gr = f_gk + gla_kw
        f_gt = f_gr + gla_vw
        assert f_gq % gla_kw == 0 and f_gr % gla_vw == 0 and f_gt % tn_merge == 0
        assert da_w % gla_vw == 0 and (da_w + gla_vw) % LANES == 0

        a = _prenorm(h, mix_norm_g[l], tm)
        qk = _rope_proj(a, w_rope, cos_t, sin_t, tm, tiles_per_batch, da_hd ** -0.5)
        ubf = _proj(a, w_bf, tm, w_bf.shape[1], BF16, "proj_bf16")
        u32 = _proj(a, w_f32, tm, tn_f32, F32, "proj_f32")

        x_da = _diff_attention(qk, ubf, da_lambda[l].astype(F32), kbias, da_subln_g[l], batch, t_pad, tq, lam_init)
        x_cv = _conv_module(u32, conv_dw_w[l], conv_dw_b[l], conv_ln_g[l], conv_ln_b[l], batch, t_pad, tt_conv)

        gate_rows = jnp.zeros((LANES, gla_kw), F32)
        wg_f = gate_rows.at[:GLA_GATE_RANK].set(gla_gate_w_fwd[l]).astype(BF16)
        wg_b = gate_rows.at[GLA_GATE_RANK:2 * GLA_GATE_RANK].set(gla_gate_w_bwd[l]).astype(BF16)
        o_f, o_b = _gla_scans(u32, ubf, wg_f, gla_gate_b_fwd[l], wg_b, gla_gate_b_bwd[l], batch, t_pad,
                              gla_kw, gla_vw, f_gq // gla_kw, f_gk // gla_kw, da_w // gla_vw,
                              (da_w + gla_vw) // LANES)
        x_gla = _gla_post(o_f, o_b, u32, gla_norm_g[l], tm, f_gr // gla_vw)

        merged = _merge(x_da, x_cv, x_gla, w_da_proj[l].astype(BF16), w_conv_proj[l].astype(BF16),
                        w_gla_proj[l].astype(BF16), b_conv_proj[l], u32, tm, tn_merge, f_gt)
        h = _out_proj(merged, w_out[l].astype(BF16), h, tm, 1024, tiles_per_batch)
        h = _mlp(h, mlp_norm_g[l], w_mlp_in[l].astype(BF16), w_mlp_out[l].astype(BF16), tm, tf)

    return _final_norm(h.reshape(batch, t_pad, d), final_norm_g, seq)
```

```python
import functools
import math

import jax
import jax.numpy as jnp
from jax import lax
from jax.experimental import pallas as pl
from jax.experimental.pallas import tpu as pltpu

F32 = jnp.float32
BF16 = jnp.bfloat16

EPS = 1e-6
ROPE_THETA = 10000.0
N_META = 16
N_BRANCH = 3
DA_HEADS = 4
GLA_HEADS = 4
GLA_GATE_RANK = 16
GLA_TAU = 16.0
CONV_WIDTH = 31

LANES = 128
SUBLANES = 8
PAD_FRONT = LANES - N_META
GLA_BLOCK = 128
GLA_SUB = 64
CONV_HALO = 16
VMEM_LIMIT = 56 * 1024 * 1024


def _cparams(*sem):
    return pltpu.CompilerParams(dimension_semantics=sem, vmem_limit_bytes=VMEM_LIMIT)


def _row_tile(t_pad):
    for n in range(1, t_pad // 16 + 1):
        if t_pad % n == 0 and (t_pad // n) % 16 == 0 and t_pad // n <= 1152:
            return t_pad // n
    raise ValueError(f"no row tile for padded sequence {t_pad}")


def _largest_divisor(n, candidates):
    for c in candidates:
        if n % c == 0:
            return c
    raise ValueError(f"no tile among {candidates} divides {n}")


def _prenorm_kernel(h_ref, g_ref, o_ref):
    x = h_ref[...]
    ms = jnp.mean(x * x, axis=-1, keepdims=True)
    o_ref[...] = (x * lax.rsqrt(ms + EPS) * g_ref[...]).astype(o_ref.dtype)


def _prenorm(h, g, tm):
    m, d = h.shape
    return pl.pallas_call(
        _prenorm_kernel,
        grid=(m // tm,),
        in_specs=[pl.BlockSpec((tm, d), lambda i: (i, 0)),
                  pl.BlockSpec((1, d), lambda i: (0, 0))],
        out_specs=pl.BlockSpec((tm, d), lambda i: (i, 0)),
        out_shape=jax.ShapeDtypeStruct((m, d), BF16),
        compiler_params=_cparams("parallel"),
        name="prenorm",
    )(h, g.reshape(1, d))


def _proj_kernel(a_ref, w_ref, o_ref):
    o_ref[...] = jnp.dot(a_ref[...], w_ref[...], preferred_element_type=F32).astype(o_ref.dtype)


def _proj(a, w, tm, tn, out_dtype, name):
    m, k = a.shape
    n = w.shape[1]
    return pl.pallas_call(
        _proj_kernel,
        grid=(m // tm, n // tn),
        in_specs=[pl.BlockSpec((tm, k), lambda i, j: (i, 0)),
                  pl.BlockSpec((k, tn), lambda i, j: (0, j))],
        out_specs=pl.BlockSpec((tm, tn), lambda i, j: (i, j)),
        out_shape=jax.ShapeDtypeStruct((m, n), out_dtype),
        compiler_params=_cparams("parallel", "arbitrary"),
        name=name,
    )(a, w)


def _rope_proj_kernel(a_ref, w_ref, cos_ref, sin_ref, o_ref, *, q_scale):
    acc = jnp.dot(a_ref[...], w_ref[...], preferred_element_type=F32)
    scale = jnp.where(pl.program_id(1) == 0, q_scale, 1.0).astype(F32)
    cos = cos_ref[...]
    sin = sin_ref[...]
    for g in range(acc.shape[1] // LANES):
        x = acc[:, g * LANES:(g + 1) * LANES]
        swapped = pltpu.roll(x, LANES // 2, axis=1)
        o_ref[:, g * LANES:(g + 1) * LANES] = ((x * cos + swapped * sin) * scale).astype(o_ref.dtype)


def _rope_proj(a, w, cos, sin, tm, tiles_per_batch, q_scale):
    m, k = a.shape
    n = w.shape[1]
    tn = n // 2
    return pl.pallas_call(
        functools.partial(_rope_proj_kernel, q_scale=q_scale),
        grid=(m // tm, 2),
        in_specs=[pl.BlockSpec((tm, k), lambda i, j: (i, 0)),
                  pl.BlockSpec((k, tn), lambda i, j: (0, j)),
                  pl.BlockSpec((tm, LANES), lambda i, j: (i % tiles_per_batch, 0)),
                  pl.BlockSpec((tm, LANES), lambda i, j: (i % tiles_per_batch, 0))],
        out_specs=pl.BlockSpec((tm, tn), lambda i, j: (i, j)),
        out_shape=jax.ShapeDtypeStruct((m, n), BF16),
        compiler_params=_cparams("parallel", "arbitrary"),
        name="rope_proj",
    )(a, w, cos, sin)


def _attn_kernel(lam_ref, q_ref, k_ref, vt_ref, g_ref, o_ref, s_scr, acc_scr, *, lam_init, tk):
    tq = q_ref.shape[0]
    d = q_ref.shape[1] // 2
    t_pad = k_ref.shape[0]
    chunks = [(0, LANES)] + [(r, tk) for r in range(LANES, t_pad, tk)]

    def fold8(x):
        return x.reshape(x.shape[0] // 8, 8, tq)

    row_ok = lax.broadcasted_iota(jnp.int32, (LANES, tq), 0) >= PAD_FRONT
    col_max = []
    for m in range(2):
        q = q_ref[:, m * d:(m + 1) * d]
        part = None
        for r0, n in chunks:
            s = lax.dot_general(k_ref[r0:r0 + n, m * d:(m + 1) * d], q, (((1,), (1,)), ((), ())),
                                preferred_element_type=F32)
            if r0 == 0:
                s = jnp.where(row_ok, s, -1e30)
            s_scr[m, r0:r0 + n, :] = s
            cmax = jnp.max(fold8(s), axis=0)
            part = cmax if part is None else jnp.maximum(part, cmax)
        col_max.append(jnp.max(part, axis=0, keepdims=True))

    col_sum = []
    for m in range(2):
        part = None
        for r0, n in chunks:
            e = jnp.exp2(s_scr[m, r0:r0 + n, :] - col_max[m])
            csum = jnp.sum(fold8(e), axis=0)
            part = csum if part is None else part + csum
            pv = jnp.dot(vt_ref[:, r0:r0 + n], e.astype(BF16), preferred_element_type=F32)
            if r0 == 0:
                acc_scr[m] = pv
            else:
                acc_scr[m] += pv
        col_sum.append(jnp.sum(part, axis=0, keepdims=True))

    lam4 = lam_ref[...]
    dot1 = jnp.sum(lam4[0:1] * lam4[1:2], axis=-1, keepdims=True)
    dot2 = jnp.sum(lam4[2:3] * lam4[3:4], axis=-1, keepdims=True)
    lam = jnp.exp(dot1) - jnp.exp(dot2) + lam_init
    ot = acc_scr[0] * (1.0 / col_sum[0]) - acc_scr[1] * (lam / col_sum[1])
    o = ot.T
    o = o * lax.rsqrt(jnp.mean(o * o, axis=-1, keepdims=True) + EPS) * g_ref[...]
    o_ref[...] = (o * (1.0 - lam_init)).astype(o_ref.dtype)


def _diff_attention(qk, vt, lam4, subln_g, batch, t_pad, tq, tk, lam_init):
    m = qk.shape[0]
    hw = 2 * lam4.shape[1]
    nq = t_pad // tq
    return pl.pallas_call(
        functools.partial(_attn_kernel, lam_init=lam_init, tk=tk),
        grid=(batch, DA_HEADS, nq),
        in_specs=[pl.BlockSpec(lam4.shape, lambda b, h, i: (0, 0)),
                  pl.BlockSpec((tq, hw), lambda b, h, i: (b * nq + i, h)),
                  pl.BlockSpec((t_pad, hw), lambda b, h, i: (b, DA_HEADS + h)),
                  pl.BlockSpec((hw, t_pad), lambda b, h, i: (b * DA_HEADS + h, 0)),
                  pl.BlockSpec((1, hw), lambda b, h, i: (0, 0))],
        out_specs=pl.BlockSpec((tq, hw), lambda b, h, i: (b * nq + i, h)),
        out_shape=jax.ShapeDtypeStruct((m, DA_HEADS * hw), BF16),
        scratch_shapes=[pltpu.VMEM((2, t_pad, tq), F32), pltpu.VMEM((2, hw, tq), F32)],
        compiler_params=_cparams("parallel", "parallel", "arbitrary"),
        name="diff_attention",
    )(lam4, qk, qk, vt, subln_g.reshape(1, hw))


def _conv_kernel(prev_ref, cur_ref, next_ref, w_ref, b_ref, lg_ref, lb_ref, o_ref, z_scr, zs_scr, *, n_tiles):
    i = pl.program_id(1)
    c = o_ref.shape[1]
    tt = o_ref.shape[0]

    def glu(u):
        return u[:, :c] * jax.nn.sigmoid(u[:, c:])

    z_scr[0:CONV_HALO, :] = jnp.where(i > 0, glu(prev_ref[...]), 0.0)
    z_scr[CONV_HALO:CONV_HALO + tt, :] = glu(cur_ref[...])
    z_scr[CONV_HALO + tt:, :] = jnp.where(i < n_tiles - 1, glu(next_ref[...]), 0.0)
    n_shift = zs_scr.shape[1]
    for s in range(1, SUBLANES):
        zs_scr[s - 1] = z_scr[s:s + n_shift, :]
    half = CONV_WIDTH // 2
    w = w_ref[...]
    acc = jnp.zeros((tt, c), F32)
    for k in range(CONV_WIDTH):
        base, s = divmod(CONV_HALO - half + k, SUBLANES)
        src = z_scr if s == 0 else zs_scr.at[s - 1]
        acc = acc + src[base * SUBLANES:base * SUBLANES + tt, :] * w[k:k + 1, :]
    y = acc + b_ref[...]
    mu = jnp.mean(y, axis=-1, keepdims=True)
    yc = y - mu
    var = jnp.mean(yc * yc, axis=-1, keepdims=True)
    y = yc * lax.rsqrt(var + EPS) * lg_ref[...] + lb_ref[...]
    o_ref[...] = (y * jax.nn.sigmoid(y)).astype(o_ref.dtype)


def _conv_module(u32, dw_w, dw_b, ln_g, ln_b, batch, t_pad, tt):
    m = u32.shape[0]
    c = dw_w.shape[1]
    nt = t_pad // tt
    hb = tt // CONV_HALO
    last_halo = m // CONV_HALO - 1
    return pl.pallas_call(
        functools.partial(_conv_kernel, n_tiles=nt),
        grid=(batch, nt),
        in_specs=[pl.BlockSpec((CONV_HALO, 2 * c), lambda b, i: (jnp.maximum((b * nt + i) * hb - 1, 0), 0)),
                  pl.BlockSpec((tt, 2 * c), lambda b, i: (b * nt + i, 0)),
                  pl.BlockSpec((CONV_HALO, 2 * c), lambda b, i: (jnp.minimum((b * nt + i + 1) * hb, last_halo), 0)),
                  pl.BlockSpec((CONV_WIDTH, c), lambda b, i: (0, 0)),
                  pl.BlockSpec((1, c), lambda b, i: (0, 0)),
                  pl.BlockSpec((1, c), lambda b, i: (0, 0)),
                  pl.BlockSpec((1, c), lambda b, i: (0, 0))],
        out_specs=pl.BlockSpec((tt, c), lambda b, i: (b * nt + i, 0)),
        out_shape=jax.ShapeDtypeStruct((m, c), BF16),
        scratch_shapes=[pltpu.VMEM((tt + 2 * CONV_HALO, c), F32),
                        pltpu.VMEM((SUBLANES - 1, tt + 2 * CONV_HALO - SUBLANES, c), F32)],
        compiler_params=_cparams("parallel", "arbitrary"),
        name="conv_module",
    )(u32, u32, u32, dw_w, dw_b.reshape(1, c), ln_g.reshape(1, c), ln_b.reshape(1, c))


def _log_sigmoid(x):
    return jnp.minimum(x, 0.0) - jnp.log(1.0 + jnp.exp(-jnp.abs(x)))


def _gla_direction(q_ref, k_ref, v_ref, lr_ref, wg_ref, bg_ref, o_ref, s_scr, *, reverse, q_scale):
    rows_total = q_ref.shape[0]
    heads = s_scr.shape[0]
    dv, dk = s_scr.shape[1], s_scr.shape[2]
    ri = lax.broadcasted_iota(jnp.int32, (GLA_SUB, GLA_SUB), 0)
    ci = lax.broadcasted_iota(jnp.int32, (GLA_SUB, GLA_SUB), 1)
    keep = (ci >= ri) if reverse else (ci <= ri)
    tri = jnp.where(keep, 1.0, 0.0).astype(BF16)
    edge = 0 if reverse else GLA_SUB - 1
    gate = jnp.dot(lr_ref[...], wg_ref[...], preferred_element_type=F32) + bg_ref[...]
    log_a = _log_sigmoid(gate) * (1.0 / GLA_TAU)
    subs = range(rows_total // GLA_SUB)
    for s in (reversed(subs) if reverse else subs):
        r0 = s * GLA_SUB
        la = log_a[r0:r0 + GLA_SUB, :]
        la_hi = la.astype(BF16)
        la_lo = (la - la_hi.astype(F32)).astype(BF16)
        b = (jnp.dot(tri, la_hi, preferred_element_type=F32)
             + jnp.dot(tri, la_lo, preferred_element_type=F32))
        b_edge = b[edge:edge + 1, :]
        eq = jnp.exp(b)
        ek = jnp.exp(-b)
        es = jnp.exp(b_edge - b)
        ed = jnp.exp(b_edge)
        for h in range(heads):
            cs = slice(h * dk, (h + 1) * dk)
            vs = slice(h * dv, (h + 1) * dv)
            q = q_ref[r0:r0 + GLA_SUB, cs] * q_scale
            k = k_ref[r0:r0 + GLA_SUB, cs]
            v = v_ref[r0:r0 + GLA_SUB, vs]
            qt = (q * eq[:, cs]).astype(BF16)
            kt = (k * ek[:, cs]).astype(BF16)
            ks = (k * es[:, cs]).astype(BF16)
            st = s_scr[h]
            o_inter = lax.dot_general(qt, st.astype(BF16), (((1,), (1,)), ((), ())),
                                      preferred_element_type=F32)
            a = lax.dot_general(qt, kt, (((1,), (1,)), ((), ())), preferred_element_type=F32)
            a = jnp.where(keep, a, 0.0).astype(BF16)
            o_ref[r0:r0 + GLA_SUB, vs] = o_inter + jnp.dot(a, v, preferred_element_type=F32)
            ds = lax.dot_general(v, ks, (((0,), (0,)), ((), ())), preferred_element_type=F32)
            s_scr[h] = st * ed[:, cs] + ds


def _gla_kernel(qf, kf, vf, lrf, qb, kb, vb, lrb, wgf, bgf, wgb, bgb, of_ref, ob_ref, sf_scr, sb_scr,
                *, q_scale):
    @pl.when(pl.program_id(1) == 0)
    def _():
        sf_scr[...] = jnp.zeros_like(sf_scr)
        sb_scr[...] = jnp.zeros_like(sb_scr)

    _gla_direction(qf, kf, vf, lrf, wgf, bgf, of_ref, sf_scr, reverse=False, q_scale=q_scale)
    _gla_direction(qb, kb, vb, lrb, wgb, bgb, ob_ref, sb_scr, reverse=True, q_scale=q_scale)


def _gla_scans(u32, ubf, ulr, wg_f, bg_f, wg_b, bg_b, batch, t_pad, kw, vw, col_q, col_k, col_v):
    m = u32.shape[0]
    nc = t_pad // GLA_BLOCK
    dk = kw // GLA_HEADS
    dv = vw // GLA_HEADS

    def fwd(col):
        return lambda b, c: (b * nc + c, col)

    def bwd(col):
        return lambda b, c: (b * nc + nc - 1 - c, col)

    def direction_specs(idx):
        return [pl.BlockSpec((GLA_BLOCK, kw), idx(col_q)),
                pl.BlockSpec((GLA_BLOCK, kw), idx(col_k)),
                pl.BlockSpec((GLA_BLOCK, vw), idx(col_v)),
                pl.BlockSpec((GLA_BLOCK, LANES), idx(0))]

    const = lambda b, c: (0, 0)
    gate_specs = [pl.BlockSpec((LANES, kw), const), pl.BlockSpec((1, kw), const),
                  pl.BlockSpec((LANES, kw), const), pl.BlockSpec((1, kw), const)]
    return pl.pallas_call(
        functools.partial(_gla_kernel, q_scale=dk ** -0.5),
        grid=(batch, nc),
        in_specs=direction_specs(fwd) + direction_specs(bwd) + gate_specs,
        out_specs=[pl.BlockSpec((GLA_BLOCK, vw), fwd(0)), pl.BlockSpec((GLA_BLOCK, vw), bwd(0))],
        out_shape=[jax.ShapeDtypeStruct((m, vw), F32), jax.ShapeDtypeStruct((m, vw), F32)],
        scratch_shapes=[pltpu.VMEM((GLA_HEADS, dv, dk), F32), pltpu.VMEM((GLA_HEADS, dv, dk), F32)],
        compiler_params=_cparams("parallel", "arbitrary"),
        name="gla_scans",
    )(u32, u32, ubf, ulr, u32, u32, ubf, ulr, wg_f, bg_f.reshape(1, kw), wg_b, bg_b.reshape(1, kw))


def _gla_post_kernel(of_ref, ob_ref, r_ref, g_ref, y_ref):
    dv = g_ref.shape[1]
    for h in range(y_ref.shape[1] // dv):
        vs = slice(h * dv, (h + 1) * dv)
        o = of_ref[:, vs] + ob_ref[:, vs]
        o = o * lax.rsqrt(jnp.mean(o * o, axis=-1, keepdims=True) + EPS) * g_ref[...]
        r = r_ref[:, vs].astype(F32)
        y_ref[:, vs] = (o * (r * jax.nn.sigmoid(r))).astype(y_ref.dtype)


def _gla_post(o_f, o_b, ubf, norm_g, tm, col_r):
    m, vw = o_f.shape
    dv = norm_g.shape[0]
    return pl.pallas_call(
        _gla_post_kernel,
        grid=(m // tm,),
        in_specs=[pl.BlockSpec((tm, vw), lambda i: (i, 0)),
                  pl.BlockSpec((tm, vw), lambda i: (i, 0)),
                  pl.BlockSpec((tm, vw), lambda i: (i, col_r)),
                  pl.BlockSpec((1, dv), lambda i: (0, 0))],
        out_specs=pl.BlockSpec((tm, vw), lambda i: (i, 0)),
        out_shape=jax.ShapeDtypeStruct((m, vw), BF16),
        compiler_params=_cparams("parallel"),
        name="gla_post",
    )(o_f, o_b, ubf, norm_g.reshape(1, dv))


def _merge_kernel(xa_ref, xc_ref, xg_ref, wa_ref, wc_ref, wg_ref, bc_ref, ga_ref, gc_ref, gg_ref, o_ref):
    ya = jnp.dot(xa_ref[...], wa_ref[...], preferred_element_type=F32)
    yc = jnp.dot(xc_ref[...], wc_ref[...], preferred_element_type=F32) + bc_ref[...]
    yg = jnp.dot(xg_ref[...], wg_ref[...], preferred_element_type=F32)
    merged = (jax.nn.sigmoid(ga_ref[...].astype(F32)) * ya + jax.nn.sigmoid(gc_ref[...].astype(F32)) * yc
              + jax.nn.sigmoid(gg_ref[...].astype(F32)) * yg)
    o_ref[...] = merged.astype(o_ref.dtype)


def _merge(x_da, x_cv, x_gla, w_da, w_cv, w_gla, b_cv, ubf, tm, tn, gate_col0):
    m, kdim = x_da.shape
    d = w_da.shape[1]
    nj = d // tn
    x_spec = pl.BlockSpec((tm, kdim), lambda i, j: (i, 0))
    w_spec = pl.BlockSpec((kdim, tn), lambda i, j: (0, j))

    def gate_spec(branch):
        return pl.BlockSpec((tm, tn), lambda i, j: (i, gate_col0 // tn + branch * nj + j))

    return pl.pallas_call(
        _merge_kernel,
        grid=(m // tm, nj),
        in_specs=[x_spec, x_spec, x_spec, w_spec, w_spec, w_spec,
                  pl.BlockSpec((1, tn), lambda i, j: (0, j)),
                  gate_spec(0), gate_spec(1), gate_spec(2)],
        out_specs=pl.BlockSpec((tm, tn), lambda i, j: (i, j)),
        out_shape=jax.ShapeDtypeStruct((m, d), BF16),
        compiler_params=_cparams("parallel", "arbitrary"),
        name="branch_merge",
    )(x_da, x_cv, x_gla, w_da, w_cv, w_gla, b_cv.reshape(1, d), ubf, ubf, ubf)


def _out_proj_kernel(x_ref, w_ref, h_ref, o_ref, *, tiles_per_batch):
    y = jnp.dot(x_ref[...], w_ref[...], preferred_element_type=F32)
    row = lax.broadcasted_iota(jnp.int32, y.shape, 0)
    first = pl.program_id(0) % tiles_per_batch == 0
    y = jnp.where(jnp.logical_and(first, row < PAD_FRONT), 0.0, y)
    o_ref[...] = h_ref[...] + y


def _out_proj(x, w, h, tm, tn, tiles_per_batch):
    m, kdim = x.shape
    d = w.shape[1]
    return pl.pallas_call(
        functools.partial(_out_proj_kernel, tiles_per_batch=tiles_per_batch),
        grid=(m // tm, d // tn),
        in_specs=[pl.BlockSpec((tm, kdim), lambda i, j: (i, 0)),
                  pl.BlockSpec((kdim, tn), lambda i, j: (0, j)),
                  pl.BlockSpec((tm, tn), lambda i, j: (i, j))],
        out_specs=pl.BlockSpec((tm, tn), lambda i, j: (i, j)),
        out_shape=jax.ShapeDtypeStruct((m, d), F32),
        compiler_params=_cparams("parallel", "arbitrary"),
        name="out_proj",
    )(x, w, h)


def _mlp_kernel(h_ref, g_ref, w1_ref, w2_ref, o_ref, a_scr):
    @pl.when(pl.program_id(1) == 0)
    def _():
        x = h_ref[...]
        ms = jnp.mean(x * x, axis=-1, keepdims=True)
        a_scr[...] = (x * lax.rsqrt(ms + EPS) * g_ref[...]).astype(a_scr.dtype)
        o_ref[...] = x

    t = jnp.maximum(jnp.dot(a_scr[...], w1_ref[...], preferred_element_type=F32), 0.0)
    o_ref[...] += jnp.dot((t * t).astype(BF16), w2_ref[...], preferred_element_type=F32)


def _mlp(h, g, w1, w2, tm, tf):
    m, d = h.shape
    ff = w1.shape[1]
    return pl.pallas_call(
        _mlp_kernel,
        grid=(m // tm, ff // tf),
        in_specs=[pl.BlockSpec((tm, d), lambda i, f: (i, 0), pipeline_mode=pl.Buffered(1)),
                  pl.BlockSpec((1, d), lambda i, f: (0, 0)),
                  pl.BlockSpec((d, tf), lambda i, f: (0, f)),
                  pl.BlockSpec((tf, d), lambda i, f: (f, 0))],
        out_specs=pl.BlockSpec((tm, d), lambda i, f: (i, 0)),
        out_shape=jax.ShapeDtypeStruct((m, d), F32),
        scratch_shapes=[pltpu.VMEM((tm, d), BF16)],
        compiler_params=_cparams("parallel", "arbitrary"),
        name="mlp",
    )(h, g.reshape(1, d), w1, w2)


def _final_norm_kernel(h_ref, g_ref, o_ref):
    x = h_ref[0]
    ms = jnp.mean(x * x, axis=-1, keepdims=True)
    o_ref[0] = x * lax.rsqrt(ms + EPS) * g_ref[...]


def _final_norm(h3, g, seq):
    batch, _, d = h3.shape
    tt = LANES
    return pl.pallas_call(
        _final_norm_kernel,
        grid=(batch, seq // tt),
        in_specs=[pl.BlockSpec((1, tt, d), lambda b, i: (b, i + 1, 0)),
                  pl.BlockSpec((1, d), lambda b, i: (0, 0))],
        out_specs=pl.BlockSpec((1, tt, d), lambda b, i: (b, i, 0)),
        out_shape=jax.ShapeDtypeStruct((batch, seq, d), F32),
        compiler_params=_cparams("parallel", "parallel"),
        name="final_norm",
    )(h3, g.reshape(1, d))


def kernel(x, meta_tokens, mix_norm_g, w_in, da_lambda, da_subln_g, w_da_proj, conv_dw_w, conv_dw_b, conv_ln_g, conv_ln_b, w_conv_proj, b_conv_proj, gla_gate_w_fwd, gla_gate_b_fwd, gla_gate_w_bwd, gla_gate_b_bwd, gla_norm_g, w_gla_proj, w_out, mlp_norm_g, w_mlp_in, w_mlp_out, final_norm_g):
    batch, seq, d = x.shape
    depth = w_in.shape[0]
    assert seq % LANES == 0 and meta_tokens.shape[0] == N_META
    t_pad = PAD_FRONT + N_META + seq
    m = batch * t_pad

    da_hd = da_lambda.shape[-1]
    da_w = w_da_proj.shape[1]
    da_qk = DA_HEADS * 2 * da_hd
    conv_c = conv_dw_w.shape[-1]
    gla_kw = gla_gate_w_fwd.shape[-1]
    gla_vw = w_gla_proj.shape[1]
    d_ff = w_mlp_in.shape[-1]
    assert da_hd == LANES and da_w == DA_HEADS * 2 * da_hd

    sizes = (da_qk, da_qk, da_w, 2 * conv_c, gla_kw, gla_kw, gla_vw, 2 * GLA_GATE_RANK, gla_vw, N_BRANCH * d)
    assert sum(sizes) == w_in.shape[-1]
    off = [0]
    for s in sizes:
        off.append(off[-1] + s)
    c_dq, c_dk, c_dv, c_cu, c_gq, c_gk, c_gv, c_lr, c_gr, c_gt = off[:-1]

    tm = _row_tile(t_pad)
    tiles_per_batch = t_pad // tm
    tq = _largest_divisor(t_pad, (384, 256, 128))
    tk_attn = _largest_divisor(seq, (1024, 512, 256, 128))
    tt_conv = LANES
    tm_proj = 2 * tm if tiles_per_batch % 2 == 0 else tm
    tn_proj = 1024
    tn_merge = 512
    tf = 1024

    pos = (jnp.arange(t_pad, dtype=F32) - PAD_FRONT)[:, None]
    inv_freq = 1.0 / (ROPE_THETA ** (jnp.arange(0, da_hd, 2, dtype=F32) / da_hd))
    ang = pos * inv_freq[None, :]
    cos_t = jnp.concatenate([jnp.cos(ang), jnp.cos(ang)], axis=-1)
    sin_t = jnp.concatenate([-jnp.sin(ang), jnp.sin(ang)], axis=-1)

    h = jnp.concatenate([jnp.zeros((batch, PAD_FRONT, d), F32),
                         jnp.broadcast_to(meta_tokens[None].astype(F32), (batch, N_META, d)),
                         x.astype(F32)], axis=1).reshape(m, d)

    for l in range(depth):
        lam_init = 0.8 - 0.6 * math.exp(-0.3 * l)
        wl = w_in[l]

        def cols(c0, width):
            return wl[:, c0:c0 + width]

        lr_pad = jnp.zeros((d, LANES - 2 * GLA_GATE_RANK), wl.dtype)
        w_rope = jnp.concatenate([cols(c_dq, da_qk), cols(c_dk, da_qk)], axis=1).astype(BF16)
        w_lr = jnp.concatenate([cols(c_lr, 2 * GLA_GATE_RANK), lr_pad], axis=1).astype(BF16)
        w_bf = jnp.concatenate([cols(c_dv, da_w), cols(c_gv, gla_vw), cols(c_gr, gla_vw),
                                cols(c_gt, N_BRANCH * d)], axis=1).astype(BF16)
        w_f32 = jnp.concatenate([cols(c_cu, 2 * conv_c), cols(c_gq, gla_kw), cols(c_gk, gla_kw)],
                                axis=1).astype(BF16)
        f_gq, f_gk = 2 * conv_c, 2 * conv_c + gla_kw
        b_gv, b_gr = da_w, da_w + gla_vw
        b_gt = b_gr + gla_vw
        assert f_gq % gla_kw == 0 and b_gv % gla_vw == 0 and b_gr % gla_vw == 0 and b_gt % tn_merge == 0

        a = _prenorm(h, mix_norm_g[l], tm)
        qk = _rope_proj(a, w_rope, cos_t, sin_t, tm, tiles_per_batch, da_hd ** -0.5 * math.log2(math.e))
        ulr = _proj(a, w_lr, tm_proj, LANES, BF16, "proj_lr")
        ubf = _proj(a, w_bf, tm_proj, tn_proj, BF16, "proj_bf16")
        u32 = _proj(a, w_f32, tm_proj, tn_proj, F32, "proj_f32")

        vt = ubf[:, :da_w].reshape(batch, t_pad, da_w).transpose(0, 2, 1).reshape(batch * da_w, t_pad)
        x_da = _diff_attention(qk, vt, da_lambda[l].astype(F32), da_subln_g[l], batch, t_pad, tq, tk_attn, lam_init)
        x_cv = _conv_module(u32, conv_dw_w[l], conv_dw_b[l], conv_ln_g[l], conv_ln_b[l], batch, t_pad, tt_conv)

        gate_rows = jnp.zeros((LANES, gla_kw), F32)
        wg_f = gate_rows.at[:GLA_GATE_RANK].set(gla_gate_w_fwd[l]).astype(BF16)
        wg_b = gate_rows.at[GLA_GATE_RANK:2 * GLA_GATE_RANK].set(gla_gate_w_bwd[l]).astype(BF16)
        o_f, o_b = _gla_scans(u32, ubf, ulr, wg_f, gla_gate_b_fwd[l], wg_b, gla_gate_b_bwd[l], batch, t_pad,
                              gla_kw, gla_vw, f_gq // gla_kw, f_gk // gla_kw, b_gv // gla_vw)
        x_gla = _gla_post(o_f, o_b, ubf, gla_norm_g[l], tm, b_gr // gla_vw)

        merged = _merge(x_da, x_cv, x_gla, w_da_proj[l].astype(BF16), w_conv_proj[l].astype(BF16),
                        w_gla_proj[l].astype(BF16), b_conv_proj[l], ubf, tm, tn_merge, b_gt)
        h = _out_proj(merged, w_out[l].astype(BF16), h, tm, 1024, tiles_per_batch)
        h = _mlp(h, mlp_norm_g[l], w_mlp_in[l].astype(BF16), w_mlp_out[l].astype(BF16), tm, tf)

    return _final_norm(h.reshape(batch, t_pad, d), final_norm_g, seq)
```

```python
import functools
import math

import jax
import jax.numpy as jnp
from jax import lax
from jax.experimental import pallas as pl
from jax.experimental.pallas import tpu as pltpu

F32 = jnp.float32
BF16 = jnp.bfloat16

EPS = 1e-6
ROPE_THETA = 10000.0
N_META = 16
N_BRANCH = 3
DA_HEADS = 4
GLA_HEADS = 4
GLA_GATE_RANK = 16
GLA_TAU = 16.0
CONV_WIDTH = 31

LANES = 128
SUBLANES = 8
PAD_FRONT = LANES - N_META
GLA_BLOCK = 128
CONV_HALO = 16
CONV_ROWS, CONV_COLS = 64, 256
VMEM_LIMIT = 56 * 1024 * 1024


def _cparams(*sem):
    return pltpu.CompilerParams(dimension_semantics=sem, vmem_limit_bytes=VMEM_LIMIT)


def _row_tile(t_pad, max_rows):
    for n in range(1, t_pad // 16 + 1):
        if t_pad % n == 0 and (t_pad // n) % 16 == 0 and t_pad // n <= max_rows:
            return t_pad // n
    raise ValueError(f"no row tile for padded sequence {t_pad}")


def _largest_divisor(n, candidates):
    for c in candidates:
        if n % c == 0:
            return c
    raise ValueError(f"no tile among {candidates} divides {n}")


def _rms(x, g):
    return x * lax.rsqrt(jnp.mean(x * x, axis=-1, keepdims=True) + EPS) * g


def _prenorm_kernel(h_ref, g_ref, o_ref):
    o_ref[...] = _rms(h_ref[...], g_ref[...]).astype(o_ref.dtype)


def _prenorm(h, g, tm):
    m, d = h.shape
    return pl.pallas_call(
        _prenorm_kernel,
        grid=(m // tm,),
        in_specs=[pl.BlockSpec((tm, d), lambda i: (i, 0)),
                  pl.BlockSpec((1, d), lambda i: (0, 0))],
        out_specs=pl.BlockSpec((tm, d), lambda i: (i, 0)),
        out_shape=jax.ShapeDtypeStruct((m, d), BF16),
        compiler_params=_cparams("parallel"),
        name="prenorm",
    )(h, g.reshape(1, d))


def _proj_kernel(a_ref, w_ref, o_ref):
    o_ref[...] = jnp.dot(a_ref[...], w_ref[...], preferred_element_type=F32).astype(o_ref.dtype)


def _proj(a, w, tm, tn, out_dtype, name):
    m, k = a.shape
    n = w.shape[1]
    return pl.pallas_call(
        _proj_kernel,
        grid=(m // tm, n // tn),
        in_specs=[pl.BlockSpec((tm, k), lambda i, j: (i, 0)),
                  pl.BlockSpec((k, tn), lambda i, j: (0, j))],
        out_specs=pl.BlockSpec((tm, tn), lambda i, j: (i, j)),
        out_shape=jax.ShapeDtypeStruct((m, n), out_dtype),
        compiler_params=_cparams("parallel", "arbitrary"),
        name=name,
    )(a, w)


def _rope_proj_kernel(a_ref, w_ref, cos_ref, sin_ref, o_ref, *, q_scale):
    acc = jnp.dot(a_ref[...], w_ref[...], preferred_element_type=F32)
    cos = cos_ref[...]
    sin = sin_ref[...]
    groups = acc.shape[1] // LANES
    for g in range(groups):
        x = acc[:, g * LANES:(g + 1) * LANES]
        swapped = pltpu.roll(x, LANES // 2, axis=1)
        y = x * cos + swapped * sin
        if g < groups // 2:
            y = y * q_scale
        o_ref[:, g * LANES:(g + 1) * LANES] = y.astype(o_ref.dtype)


def _rope_proj(a, w, cos, sin, tm, tiles_per_batch, q_scale):
    m, k = a.shape
    n = w.shape[1]
    return pl.pallas_call(
        functools.partial(_rope_proj_kernel, q_scale=q_scale),
        grid=(m // tm,),
        in_specs=[pl.BlockSpec((tm, k), lambda i: (i, 0)),
                  pl.BlockSpec((k, n), lambda i: (0, 0), pipeline_mode=pl.Buffered(1)),
                  pl.BlockSpec((tm, LANES), lambda i: (i % tiles_per_batch, 0)),
                  pl.BlockSpec((tm, LANES), lambda i: (i % tiles_per_batch, 0))],
        out_specs=pl.BlockSpec((tm, n), lambda i: (i, 0)),
        out_shape=jax.ShapeDtypeStruct((m, n), BF16),
        compiler_params=_cparams("parallel"),
        name="rope_proj",
    )(a, w, cos, sin)


def _attn_kernel(lam_ref, q_ref, k_ref, vt_ref, g_ref, o_ref, s_scr, acc_scr, *, lam_init, tk):
    tq = q_ref.shape[0]
    d = q_ref.shape[1] // 2
    t_pad = k_ref.shape[0]
    chunks = [(0, LANES)] + [(r, tk) for r in range(LANES, t_pad, tk)]

    def fold8(x):
        return x.reshape(x.shape[0] // 8, 8, tq)

    row_ok = lax.broadcasted_iota(jnp.int32, (LANES, tq), 0) >= PAD_FRONT
    col_max = []
    for m in range(2):
        q = q_ref[:, m * d:(m + 1) * d]
        part = None
        for r0, n in chunks:
            s = lax.dot_general(k_ref[r0:r0 + n, m * d:(m + 1) * d], q, (((1,), (1,)), ((), ())),
                                preferred_element_type=F32)
            if r0 == 0:
                s = jnp.where(row_ok, s, -1e30)
            s_scr[m, r0:r0 + n, :] = s
            cmax = jnp.max(fold8(s), axis=0)
            part = cmax if part is None else jnp.maximum(part, cmax)
        col_max.append(jnp.max(part, axis=0, keepdims=True))

    col_sum = []
    for m in range(2):
        part = None
        for r0, n in chunks:
            e = jnp.exp2(s_scr[m, r0:r0 + n, :] - col_max[m])
            csum = jnp.sum(fold8(e), axis=0)
            part = csum if part is None else part + csum
            pv = jnp.dot(vt_ref[:, r0:r0 + n], e.astype(BF16), preferred_element_type=F32)
            if r0 == 0:
                acc_scr[m] = pv
            else:
                acc_scr[m] += pv
        col_sum.append(jnp.sum(part, axis=0, keepdims=True))

    lam4 = lam_ref[...]
    dot1 = jnp.sum(lam4[0:1] * lam4[1:2], axis=-1, keepdims=True)
    dot2 = jnp.sum(lam4[2:3] * lam4[3:4], axis=-1, keepdims=True)
    lam = jnp.exp(dot1) - jnp.exp(dot2) + lam_init
    ot = acc_scr[0] * (1.0 / col_sum[0]) - acc_scr[1] * (lam / col_sum[1])
    o = ot.T
    o = o * lax.rsqrt(jnp.mean(o * o, axis=-1, keepdims=True) + EPS) * g_ref[...]
    o_ref[...] = (o * (1.0 - lam_init)).astype(o_ref.dtype)


def _diff_attention(qk, vt, lam4, subln_g, batch, t_pad, tq, tk, lam_init):
    m = qk.shape[0]
    hw = 2 * lam4.shape[1]
    nq = t_pad // tq
    return pl.pallas_call(
        functools.partial(_attn_kernel, lam_init=lam_init, tk=tk),
        grid=(batch, DA_HEADS, nq),
        in_specs=[pl.BlockSpec(lam4.shape, lambda b, h, i: (0, 0)),
                  pl.BlockSpec((tq, hw), lambda b, h, i: (b * nq + i, h)),
                  pl.BlockSpec((t_pad, hw), lambda b, h, i: (b, DA_HEADS + h)),
                  pl.BlockSpec((hw, t_pad), lambda b, h, i: (b * DA_HEADS + h, 0)),
                  pl.BlockSpec((1, hw), lambda b, h, i: (0, 0))],
        out_specs=pl.BlockSpec((tq, hw), lambda b, h, i: (b * nq + i, h)),
        out_shape=jax.ShapeDtypeStruct((m, DA_HEADS * hw), BF16),
        scratch_shapes=[pltpu.VMEM((2, t_pad, tq), F32), pltpu.VMEM((2, hw, tq), F32)],
        compiler_params=_cparams("parallel", "parallel", "arbitrary"),
        name="diff_attention",
    )(lam4, qk, qk, vt, subln_g.reshape(1, hw))


def _conv_kernel(prev_ref, cur_ref, next_ref, w_ref, b_ref, lg_ref, lb_ref, o_ref, z_scr, zs_scr, y_scr,
                 *, n_tiles):
    i = pl.program_id(1)
    c = o_ref.shape[1]
    tt = o_ref.shape[0]

    def glu(u):
        return u[:, :c] * jax.nn.sigmoid(u[:, c:])

    z_scr[0:CONV_HALO, :] = jnp.where(i > 0, glu(prev_ref[...]), 0.0)
    z_scr[CONV_HALO:CONV_HALO + tt, :] = glu(cur_ref[...])
    z_scr[CONV_HALO + tt:, :] = jnp.where(i < n_tiles - 1, glu(next_ref[...]), 0.0)
    n_shift = zs_scr.shape[1]
    for s in range(1, SUBLANES):
        zs_scr[s - 1] = z_scr[s:s + n_shift, :]
    half = CONV_WIDTH // 2
    rows = _largest_divisor(tt, (CONV_ROWS, SUBLANES))
    for r in range(0, tt, rows):
        for c0 in range(0, c, CONV_COLS):
            acc = b_ref[:, c0:c0 + CONV_COLS]
            for k in range(CONV_WIDTH):
                base, s = divmod(CONV_HALO - half + k, SUBLANES)
                src = z_scr if s == 0 else zs_scr.at[s - 1]
                r_src = base * SUBLANES + r
                acc = acc + src[r_src:r_src + rows, c0:c0 + CONV_COLS] * w_ref[k:k + 1, c0:c0 + CONV_COLS]
            y_scr[r:r + rows, c0:c0 + CONV_COLS] = acc
    for r in range(0, tt, rows):
        y = y_scr[r:r + rows, :]
        mu = jnp.mean(y, axis=-1, keepdims=True)
        yc = y - mu
        var = jnp.mean(yc * yc, axis=-1, keepdims=True)
        y = yc * lax.rsqrt(var + EPS) * lg_ref[...] + lb_ref[...]
        o_ref[r:r + rows, :] = (y * jax.nn.sigmoid(y)).astype(o_ref.dtype)


def _conv_module(u32, dw_w, dw_b, ln_g, ln_b, batch, t_pad, tt):
    m = u32.shape[0]
    c = dw_w.shape[1]
    nt = t_pad // tt
    hb = tt // CONV_HALO
    last_halo = m // CONV_HALO - 1
    return pl.pallas_call(
        functools.partial(_conv_kernel, n_tiles=nt),
        grid=(batch, nt),
        in_specs=[pl.BlockSpec((CONV_HALO, 2 * c), lambda b, i: (jnp.maximum((b * nt + i) * hb - 1, 0), 0)),
                  pl.BlockSpec((tt, 2 * c), lambda b, i: (b * nt + i, 0)),
                  pl.BlockSpec((CONV_HALO, 2 * c), lambda b, i: (jnp.minimum((b * nt + i + 1) * hb, last_halo), 0)),
                  pl.BlockSpec((CONV_WIDTH, c), lambda b, i: (0, 0)),
                  pl.BlockSpec((1, c), lambda b, i: (0, 0)),
                  pl.BlockSpec((1, c), lambda b, i: (0, 0)),
                  pl.BlockSpec((1, c), lambda b, i: (0, 0))],
        out_specs=pl.BlockSpec((tt, c), lambda b, i: (b * nt + i, 0)),
        out_shape=jax.ShapeDtypeStruct((m, c), BF16),
        scratch_shapes=[pltpu.VMEM((tt + 2 * CONV_HALO, c), F32),
                        pltpu.VMEM((SUBLANES - 1, tt + 2 * CONV_HALO - SUBLANES, c), F32),
                        pltpu.VMEM((tt, c), F32)],
        compiler_params=_cparams("parallel", "arbitrary"),
        name="conv_module",
    )(u32, u32, u32, dw_w, dw_b.reshape(1, c), ln_g.reshape(1, c), ln_b.reshape(1, c))


def _log_sigmoid(x):
    return jnp.minimum(x, 0.0) - jnp.log(1.0 + jnp.exp(-jnp.abs(x)))


class _GlaStream:
    def __init__(self, q_ref, k_ref, v_ref, lr_ref, wg_ref, bg_ref, o_ref, s_scr, reverse):
        self.q_ref, self.k_ref, self.v_ref, self.lr_ref = q_ref, k_ref, v_ref, lr_ref
        self.wg_ref, self.bg_ref, self.o_ref, self.s_scr = wg_ref, bg_ref, o_ref, s_scr
        n = q_ref.shape[0]
        ri = lax.broadcasted_iota(jnp.int32, (n, n), 0)
        ci = lax.broadcasted_iota(jnp.int32, (n, n), 1)
        self.keep = (ci >= ri) if reverse else (ci <= ri)
        self.edge = 0 if reverse else n - 1
        self.mid = n // 2 if reverse else n // 2 - 1


def _gla_chunk(streams, q_scale):
    heads, dv, dk = streams[0].s_scr.shape
    nt = (((1,), (1,)), ((), ()))
    for s in streams:
        gate = jnp.dot(s.lr_ref[...], s.wg_ref[...], preferred_element_type=F32) + s.bg_ref[...]
        la = _log_sigmoid(gate) * (1.0 / GLA_TAU)
        la_hi = la.astype(BF16)
        s.la_split = jnp.concatenate([la_hi, (la - la_hi.astype(F32)).astype(BF16)], axis=0)
    for s in streams:
        tri = jnp.where(s.keep, 1.0, 0.0).astype(BF16)
        s.b = jnp.dot(jnp.concatenate([tri, tri], axis=1), s.la_split, preferred_element_type=F32)
    for s in streams:
        b = s.b
        b_edge = b[s.edge:s.edge + 1, :]
        b_mid = b[s.mid:s.mid + 1, :]
        q = s.q_ref[...] * q_scale
        k = s.k_ref[...]
        s.q_inter = (q * jnp.exp(b)).astype(BF16)
        s.k_state = (k * jnp.exp(b_edge - b)).astype(BF16)
        s.q_intra = (q * jnp.exp(b - b_mid)).astype(BF16)
        s.k_intra = (k * jnp.exp(b_mid - b)).astype(BF16)
        s.decay = jnp.exp(b_edge)
    for s in streams:
        s.a, s.o_inter = [], []
        for h in range(heads):
            cs = slice(h * dk, (h + 1) * dk)
            s.a.append(lax.dot_general(s.q_intra[:, cs], s.k_intra[:, cs], nt, preferred_element_type=F32))
            s.o_inter.append(lax.dot_general(s.q_inter[:, cs], s.s_scr[h].astype(BF16), nt,
                                             preferred_element_type=F32))
    for s in streams:
        for h in range(heads):
            vs = slice(h * dv, (h + 1) * dv)
            a = jnp.where(s.keep, s.a[h], 0.0).astype(BF16)
            s.o_ref[:, vs] = s.o_inter[h] + jnp.dot(a, s.v_ref[:, vs], preferred_element_type=F32)
    for s in streams:
        for h in range(heads):
            cs = slice(h * dk, (h + 1) * dk)
            ds = lax.dot_general(s.v_ref[:, h * dv:(h + 1) * dv], s.k_state[:, cs], (((0,), (0,)), ((), ())),
                                 preferred_element_type=F32)
            s.s_scr[h] = s.s_scr[h] * s.decay[:, cs] + ds


def _gla_kernel(qf, kf, vf, lrf, qb, kb, vb, lrb, wgf, bgf, wgb, bgb, of_ref, ob_ref, sf_scr, sb_scr,
                *, q_scale):
    @pl.when(pl.program_id(1) == 0)
    def _():
        sf_scr[...] = jnp.zeros_like(sf_scr)
        sb_scr[...] = jnp.zeros_like(sb_scr)

    streams = []
    for g in range(qf.shape[0]):
        streams.append(_GlaStream(qf.at[g], kf.at[g], vf.at[g], lrf.at[g], wgf, bgf, of_ref.at[g], sf_scr.at[g], False))
        streams.append(_GlaStream(qb.at[g], kb.at[g], vb.at[g], lrb.at[g], wgb, bgb, ob_ref.at[g], sb_scr.at[g], True))
    _gla_chunk(streams, q_scale)


def _gla_scans(u32, ubf, ulr, wg_f, bg_f, wg_b, bg_b, batch, t_pad, kw, vw, col_q, col_k, col_v):
    m = u32.shape[0]
    nc = t_pad // GLA_BLOCK
    dk = kw // GLA_HEADS
    dv = vw // GLA_HEADS
    group = 2 if batch % 2 == 0 else 1

    def per_batch(x):
        return x.reshape(batch, t_pad, x.shape[1])

    def fwd(col):
        return lambda b, c: (b, c, col)

    def bwd(col):
        return lambda b, c: (b, nc - 1 - c, col)

    def direction_specs(idx):
        return [pl.BlockSpec((group, GLA_BLOCK, kw), idx(col_q)),
                pl.BlockSpec((group, GLA_BLOCK, kw), idx(col_k)),
                pl.BlockSpec((group, GLA_BLOCK, vw), idx(col_v)),
                pl.BlockSpec((group, GLA_BLOCK, LANES), idx(0))]

    const = lambda b, c: (0, 0)
    gate_specs = [pl.BlockSpec((LANES, kw), const), pl.BlockSpec((1, kw), const),
                  pl.BlockSpec((LANES, kw), const), pl.BlockSpec((1, kw), const)]
    u32_3, ubf_3, ulr_3 = per_batch(u32), per_batch(ubf), per_batch(ulr)
    state = pltpu.VMEM((group, GLA_HEADS, dv, dk), F32)
    o_f, o_b = pl.pallas_call(
        functools.partial(_gla_kernel, q_scale=dk ** -0.5),
        grid=(batch // group, nc),
        in_specs=direction_specs(fwd) + direction_specs(bwd) + gate_specs,
        out_specs=[pl.BlockSpec((group, GLA_BLOCK, vw), fwd(0)), pl.BlockSpec((group, GLA_BLOCK, vw), bwd(0))],
        out_shape=[jax.ShapeDtypeStruct((batch, t_pad, vw), F32), jax.ShapeDtypeStruct((batch, t_pad, vw), F32)],
        scratch_shapes=[state, state],
        compiler_params=_cparams("parallel", "arbitrary"),
        name="gla_scans",
    )(u32_3, u32_3, ubf_3, ulr_3, u32_3, u32_3, ubf_3, ulr_3, wg_f, bg_f.reshape(1, kw), wg_b, bg_b.reshape(1, kw))
    return o_f.reshape(m, vw), o_b.reshape(m, vw)


def _gla_post_kernel(of_ref, ob_ref, r_ref, g_ref, y_ref):
    dv = g_ref.shape[1]
    for h in range(y_ref.shape[1] // dv):
        vs = slice(h * dv, (h + 1) * dv)
        o = of_ref[:, vs] + ob_ref[:, vs]
        o = o * lax.rsqrt(jnp.mean(o * o, axis=-1, keepdims=True) + EPS) * g_ref[...]
        r = r_ref[:, vs].astype(F32)
        y_ref[:, vs] = (o * (r * jax.nn.sigmoid(r))).astype(y_ref.dtype)


def _gla_post(o_f, o_b, ubf, norm_g, tm, col_r):
    m, vw = o_f.shape
    dv = norm_g.shape[0]
    return pl.pallas_call(
        _gla_post_kernel,
        grid=(m // tm,),
        in_specs=[pl.BlockSpec((tm, vw), lambda i: (i, 0)),
                  pl.BlockSpec((tm, vw), lambda i: (i, 0)),
                  pl.BlockSpec((tm, vw), lambda i: (i, col_r)),
                  pl.BlockSpec((1, dv), lambda i: (0, 0))],
        out_specs=pl.BlockSpec((tm, vw), lambda i: (i, 0)),
        out_shape=jax.ShapeDtypeStruct((m, vw), BF16),
        compiler_params=_cparams("parallel"),
        name="gla_post",
    )(o_f, o_b, ubf, norm_g.reshape(1, dv))


def _merge_kernel(xa_ref, xc_ref, xg_ref, wa_ref, wc_ref, wg_ref, bc_ref, ga_ref, gc_ref, gg_ref, o_ref):
    ya = jnp.dot(xa_ref[...], wa_ref[...], preferred_element_type=F32)
    yc = jnp.dot(xc_ref[...], wc_ref[...], preferred_element_type=F32) + bc_ref[...]
    yg = jnp.dot(xg_ref[...], wg_ref[...], preferred_element_type=F32)
    merged = (jax.nn.sigmoid(ga_ref[...].astype(F32)) * ya + jax.nn.sigmoid(gc_ref[...].astype(F32)) * yc
              + jax.nn.sigmoid(gg_ref[...].astype(F32)) * yg)
    o_ref[...] = merged.astype(o_ref.dtype)


def _merge(x_da, x_cv, x_gla, w_da, w_cv, w_gla, b_cv, ubf, tm, gate_col0):
    m, kdim = x_da.shape
    d = w_da.shape[1]
    x_spec = pl.BlockSpec((tm, kdim), lambda i: (i, 0))
    w_spec = pl.BlockSpec((kdim, d), lambda i: (0, 0), pipeline_mode=pl.Buffered(1))

    def gate_spec(branch):
        return pl.BlockSpec((tm, d), lambda i: (i, gate_col0 // d + branch))

    return pl.pallas_call(
        _merge_kernel,
        grid=(m // tm,),
        in_specs=[x_spec, x_spec, x_spec, w_spec, w_spec, w_spec,
                  pl.BlockSpec((1, d), lambda i: (0, 0)),
                  gate_spec(0), gate_spec(1), gate_spec(2)],
        out_specs=pl.BlockSpec((tm, d), lambda i: (i, 0)),
        out_shape=jax.ShapeDtypeStruct((m, d), BF16),
        compiler_params=_cparams("parallel"),
        name="branch_merge",
    )(x_da, x_cv, x_gla, w_da, w_cv, w_gla, b_cv.reshape(1, d), ubf, ubf, ubf)


def _out_proj_kernel(x_ref, w_ref, h_ref, o_ref, *, tiles_per_batch):
    y = jnp.dot(x_ref[...], w_ref[...], preferred_element_type=F32)
    row = lax.broadcasted_iota(jnp.int32, y.shape, 0)
    first = pl.program_id(0) % tiles_per_batch == 0
    y = jnp.where(jnp.logical_and(first, row < PAD_FRONT), 0.0, y)
    o_ref[...] = h_ref[...] + y


def _out_proj(x, w, h, tm, tiles_per_batch):
    m, kdim = x.shape
    d = w.shape[1]
    return pl.pallas_call(
        functools.partial(_out_proj_kernel, tiles_per_batch=tiles_per_batch),
        grid=(m // tm,),
        in_specs=[pl.BlockSpec((tm, kdim), lambda i: (i, 0)),
                  pl.BlockSpec((kdim, d), lambda i: (0, 0), pipeline_mode=pl.Buffered(1)),
                  pl.BlockSpec((tm, d), lambda i: (i, 0))],
        out_specs=pl.BlockSpec((tm, d), lambda i: (i, 0)),
        out_shape=jax.ShapeDtypeStruct((m, d), F32),
        compiler_params=_cparams("parallel"),
        name="out_proj",
    )(x, w, h)


def _mlp_kernel(h_ref, g_ref, w1_ref, w2_ref, gn_ref, *rest, final):
    if final:
        o_ref, a_scr = rest
    else:
        o_ref, an_ref, a_scr = rest

    @pl.when(pl.program_id(1) == 0)
    def _():
        x = h_ref[...]
        a_scr[...] = _rms(x, g_ref[...]).astype(a_scr.dtype)
        o_ref[...] = x

    t = jnp.maximum(jnp.dot(a_scr[...], w1_ref[...], preferred_element_type=F32), 0.0)
    o_ref[...] += jnp.dot((t * t).astype(BF16), w2_ref[...], preferred_element_type=F32)

    @pl.when(pl.program_id(1) == pl.num_programs(1) - 1)
    def _():
        y = _rms(o_ref[...], gn_ref[...])
        if final:
            o_ref[...] = y
        else:
            an_ref[...] = y.astype(an_ref.dtype)


def _mlp(h, g, w1, w2, g_next, tm, tf, final):
    m, d = h.shape
    ff = w1.shape[1]
    row_spec = pl.BlockSpec((tm, d), lambda i, f: (i, 0))
    vec_spec = pl.BlockSpec((1, d), lambda i, f: (0, 0))
    out_specs = row_spec if final else [row_spec, row_spec]
    out_shape = (jax.ShapeDtypeStruct((m, d), F32) if final
                 else [jax.ShapeDtypeStruct((m, d), F32), jax.ShapeDtypeStruct((m, d), BF16)])
    return pl.pallas_call(
        functools.partial(_mlp_kernel, final=final),
        grid=(m // tm, ff // tf),
        in_specs=[row_spec, vec_spec,
                  pl.BlockSpec((d, tf), lambda i, f: (0, f)),
                  pl.BlockSpec((tf, d), lambda i, f: (f, 0)),
                  vec_spec],
        out_specs=out_specs,
        out_shape=out_shape,
        scratch_shapes=[pltpu.VMEM((tm, d), BF16)],
        compiler_params=_cparams("parallel", "arbitrary"),
        name="mlp",
    )(h, g.reshape(1, d), w1, w2, g_next.reshape(1, d))


def kernel(x, meta_tokens, mix_norm_g, w_in, da_lambda, da_subln_g, w_da_proj, conv_dw_w, conv_dw_b, conv_ln_g, conv_ln_b, w_conv_proj, b_conv_proj, gla_gate_w_fwd, gla_gate_b_fwd, gla_gate_w_bwd, gla_gate_b_bwd, gla_norm_g, w_gla_proj, w_out, mlp_norm_g, w_mlp_in, w_mlp_out, final_norm_g):
    batch, seq, d = x.shape
    depth = w_in.shape[0]
    assert seq % LANES == 0 and meta_tokens.shape[0] == N_META
    t_pad = PAD_FRONT + N_META + seq
    m = batch * t_pad

    da_hd = da_lambda.shape[-1]
    da_w = w_da_proj.shape[1]
    da_qk = DA_HEADS * 2 * da_hd
    conv_c = conv_dw_w.shape[-1]
    gla_kw = gla_gate_w_fwd.shape[-1]
    gla_vw = w_gla_proj.shape[1]
    d_ff = w_mlp_in.shape[-1]
    assert da_hd == LANES and da_w == DA_HEADS * 2 * da_hd

    sizes = (da_qk, da_qk, da_w, 2 * conv_c, gla_kw, gla_kw, gla_vw, 2 * GLA_GATE_RANK, gla_vw, N_BRANCH * d)
    assert sum(sizes) == w_in.shape[-1]
    off = [0]
    for s in sizes:
        off.append(off[-1] + s)
    c_dq, c_dk, c_dv, c_cu, c_gq, c_gk, c_gv, c_lr, c_gr, c_gt = off[:-1]

    tm = _row_tile(t_pad, 1152)
    tm_mlp = _row_tile(t_pad, 768)
    tm_mix = _row_tile(t_pad, 576)
    tiles_per_batch = t_pad // tm
    tq = _largest_divisor(t_pad, (384, 256, 128))
    tk_attn = _largest_divisor(seq, (1024, 512, 256, 128))
    tt_conv = _largest_divisor(t_pad, (384, 256, 128))
    tm_proj = 2 * tm if tiles_per_batch % 2 == 0 else tm
    tn_proj = 1024
    tf = 1024

    pos = (jnp.arange(t_pad, dtype=F32) - PAD_FRONT)[:, None]
    inv_freq = 1.0 / (ROPE_THETA ** (jnp.arange(0, da_hd, 2, dtype=F32) / da_hd))
    ang = pos * inv_freq[None, :]
    cos_t = jnp.concatenate([jnp.cos(ang), jnp.cos(ang)], axis=-1)
    sin_t = jnp.concatenate([-jnp.sin(ang), jnp.sin(ang)], axis=-1)

    h = jnp.concatenate([jnp.zeros((batch, PAD_FRONT, d), F32),
                         jnp.broadcast_to(meta_tokens[None].astype(F32), (batch, N_META, d)),
                         x.astype(F32)], axis=1).reshape(m, d)

    for l in range(depth):
        lam_init = 0.8 - 0.6 * math.exp(-0.3 * l)
        wl = w_in[l].astype(BF16)

        def cols(c0, width):
            return wl[:, c0:c0 + width]

        lr_pad = jnp.zeros((d, LANES - 2 * GLA_GATE_RANK), wl.dtype)
        w_rope = jnp.concatenate([cols(c_dq, da_qk), cols(c_dk, da_qk)], axis=1)
        w_lr = jnp.concatenate([cols(c_lr, 2 * GLA_GATE_RANK), lr_pad], axis=1)
        w_bf = jnp.concatenate([cols(c_gt, N_BRANCH * d), cols(c_dv, da_w), cols(c_gv, gla_vw),
                                cols(c_gr, gla_vw)], axis=1)
        w_f32 = jnp.concatenate([cols(c_cu, 2 * conv_c), cols(c_gq, gla_kw), cols(c_gk, gla_kw)], axis=1)
        f_gq, f_gk = 2 * conv_c, 2 * conv_c + gla_kw
        b_gt, b_dv = 0, N_BRANCH * d
        b_gv = b_dv + da_w
        b_gr = b_gv + gla_vw
        assert f_gq % gla_kw == 0 and b_gv % gla_vw == 0 and b_gr % gla_vw == 0

        if l == 0:
            a = _prenorm(h, mix_norm_g[0], tm)
        qk = _rope_proj(a, w_rope, cos_t, sin_t, tm, tiles_per_batch, da_hd ** -0.5 * math.log2(math.e))
        ulr = _proj(a, w_lr, tm_proj, LANES, BF16, "proj_lr")
        ubf = _proj(a, w_bf, tm_proj, tn_proj, BF16, "proj_bf16")
        u32 = _proj(a, w_f32, tm_proj, tn_proj, F32, "proj_f32")

        vt = ubf[:, b_dv:b_dv + da_w].reshape(batch, t_pad, da_w).transpose(0, 2, 1).reshape(batch * da_w, t_pad)
        x_da = _diff_attention(qk, vt, da_lambda[l].astype(F32), da_subln_g[l], batch, t_pad, tq, tk_attn, lam_init)
        x_cv = _conv_module(u32, conv_dw_w[l], conv_dw_b[l], conv_ln_g[l], conv_ln_b[l], batch, t_pad, tt_conv)

        gate_rows = jnp.zeros((LANES, gla_kw), F32)
        wg_f = gate_rows.at[:GLA_GATE_RANK].set(gla_gate_w_fwd[l]).astype(BF16)
        wg_b = gate_rows.at[GLA_GATE_RANK:2 * GLA_GATE_RANK].set(gla_gate_w_bwd[l]).astype(BF16)
        o_f, o_b = _gla_scans(u32, ubf, ulr, wg_f, gla_gate_b_fwd[l], wg_b, gla_gate_b_bwd[l], batch, t_pad,
                              gla_kw, gla_vw, f_gq // gla_kw, f_gk // gla_kw, b_gv // gla_vw)
        x_gla = _gla_post(o_f, o_b, ubf, gla_norm_g[l], tm, b_gr // gla_vw)

        merged = _merge(x_da, x_cv, x_gla, w_da_proj[l].astype(BF16), w_conv_proj[l].astype(BF16),
                        w_gla_proj[l].astype(BF16), b_conv_proj[l], ubf, tm_mix, b_gt)
        h = _out_proj(merged, w_out[l].astype(BF16), h, tm_mix, t_pad // tm_mix)
        w1, w2 = w_mlp_in[l].astype(BF16), w_mlp_out[l].astype(BF16)
        if l + 1 < depth:
            h, a = _mlp(h, mlp_norm_g[l], w1, w2, mix_norm_g[l + 1], tm_mlp, tf, False)
        else:
            out = _mlp(h, mlp_norm_g[l], w1, w2, final_norm_g, tm_mlp, tf, True)

    return out.reshape(batch, t_pad, d)[:, PAD_FRONT + N_META:, :]
```

```python
import functools
import math

import jax
import jax.numpy as jnp
import numpy as np
from jax import lax
from jax.experimental import pallas as pl
from jax.experimental.pallas import tpu as pltpu

F32 = jnp.float32
BF16 = jnp.bfloat16

EPS = 1e-6
ROPE_THETA = 10000.0
N_META = 16
N_BRANCH = 3
DA_HEADS = 4
GLA_HEADS = 4
GLA_GATE_RANK = 16
GLA_TAU = 16.0
CONV_WIDTH = 31

LANES = 128
SUBLANES = 8
PAD_FRONT = LANES - N_META
GLA_BLOCK = 128
GLA_SAFE_LOG_DECAY = 80.0
CONV_HALO = 16
CONV_ROWS, CONV_COLS = 64, 256
VMEM_LIMIT = 56 * 1024 * 1024


def _cparams(*sem):
    return pltpu.CompilerParams(dimension_semantics=sem, vmem_limit_bytes=VMEM_LIMIT)


def _row_tile(t_pad, max_rows):
    for n in range(1, t_pad // 16 + 1):
        if t_pad % n == 0 and (t_pad // n) % 16 == 0 and t_pad // n <= max_rows:
            return t_pad // n
    raise ValueError(f"no row tile for padded sequence {t_pad}")


def _largest_divisor(n, candidates):
    for c in candidates:
        if n % c == 0:
            return c
    raise ValueError(f"no tile among {candidates} divides {n}")


def _rms(x, g):
    return x * lax.rsqrt(jnp.mean(x * x, axis=-1, keepdims=True) + EPS) * g


def _embed_kernel(x_ref, meta_ref, g_ref, h_ref, a_ref):
    @pl.when(pl.program_id(1) == 0)
    def _():
        h_ref[0, 0:PAD_FRONT, :] = jnp.zeros((PAD_FRONT, h_ref.shape[2]), h_ref.dtype)
        h_ref[0, PAD_FRONT:, :] = meta_ref[...]

    @pl.when(pl.program_id(1) > 0)
    def _():
        h_ref[0] = x_ref[0]

    a_ref[0] = _rms(h_ref[0], g_ref[...]).astype(a_ref.dtype)


def _embed(x, meta_tokens, g):
    batch, seq, d = x.shape
    t_pad = LANES + seq
    row_spec = pl.BlockSpec((1, LANES, d), lambda b, i: (b, i, 0))
    h, a = pl.pallas_call(
        _embed_kernel,
        grid=(batch, t_pad // LANES),
        in_specs=[pl.BlockSpec((1, LANES, d), lambda b, i: (b, jnp.maximum(i - 1, 0), 0)),
                  pl.BlockSpec((N_META, d), lambda b, i: (0, 0)),
                  pl.BlockSpec((1, d), lambda b, i: (0, 0))],
        out_specs=[row_spec, row_spec],
        out_shape=[jax.ShapeDtypeStruct((batch, t_pad, d), F32), jax.ShapeDtypeStruct((batch, t_pad, d), BF16)],
        compiler_params=_cparams("parallel", "arbitrary"),
        name="embed",
    )(x, meta_tokens, g.reshape(1, d))
    return h.reshape(batch * t_pad, d), a.reshape(batch * t_pad, d)


def _proj_kernel(a_ref, w_ref, o_ref):
    o_ref[...] = jnp.dot(a_ref[...], w_ref[...], preferred_element_type=F32).astype(o_ref.dtype)


def _proj(a, w, tm, tn, out_dtype, name, n_tiles, col_block):
    m, k = a.shape
    return pl.pallas_call(
        _proj_kernel,
        grid=(m // tm, n_tiles),
        in_specs=[pl.BlockSpec((tm, k), lambda i, j: (i, 0)),
                  pl.BlockSpec((k, tn), lambda i, j: (0, col_block(j)))],
        out_specs=pl.BlockSpec((tm, tn), lambda i, j: (i, j)),
        out_shape=jax.ShapeDtypeStruct((m, n_tiles * tn), out_dtype),
        compiler_params=_cparams("parallel", "arbitrary"),
        name=name,
    )(a, w)


def _rope_proj_kernel(a_ref, w_ref, cos_ref, sin_ref, o_ref, *, q_scale):
    acc = jnp.dot(a_ref[...], w_ref[...], preferred_element_type=F32)
    cos = cos_ref[...]
    sin = sin_ref[...]
    groups = acc.shape[1] // LANES
    for g in range(groups):
        x = acc[:, g * LANES:(g + 1) * LANES]
        swapped = pltpu.roll(x, LANES // 2, axis=1)
        y = x * cos + swapped * sin
        if g < groups // 2:
            y = y * q_scale
        o_ref[:, g * LANES:(g + 1) * LANES] = y.astype(o_ref.dtype)


def _rope_proj(a, w, n, cos, sin, tm, tiles_per_batch, q_scale):
    m, k = a.shape
    return pl.pallas_call(
        functools.partial(_rope_proj_kernel, q_scale=q_scale),
        grid=(m // tm,),
        in_specs=[pl.BlockSpec((tm, k), lambda i: (i, 0)),
                  pl.BlockSpec((k, n), lambda i: (0, 0), pipeline_mode=pl.Buffered(1)),
                  pl.BlockSpec((tm, LANES), lambda i: (i % tiles_per_batch, 0)),
                  pl.BlockSpec((tm, LANES), lambda i: (i % tiles_per_batch, 0))],
        out_specs=pl.BlockSpec((tm, n), lambda i: (i, 0)),
        out_shape=jax.ShapeDtypeStruct((m, n), BF16),
        compiler_params=_cparams("parallel"),
        name="rope_proj",
    )(a, w, cos, sin)


def _attn_kernel(lam_ref, q_ref, k_ref, vt_ref, g_ref, o_ref, s_scr, acc_scr, *, lam_init, tk):
    tq = q_ref.shape[0]
    d = q_ref.shape[1] // 2
    t_pad = k_ref.shape[0]
    chunks = [(0, LANES)] + [(r, tk) for r in range(LANES, t_pad, tk)]

    def fold8(x):
        return x.reshape(x.shape[0] // 8, 8, tq)

    row_ok = lax.broadcasted_iota(jnp.int32, (LANES, tq), 0) >= PAD_FRONT
    col_max = []
    for m in range(2):
        q = q_ref[:, m * d:(m + 1) * d]
        part = None
        for r0, n in chunks:
            s = lax.dot_general(k_ref[r0:r0 + n, m * d:(m + 1) * d], q, (((1,), (1,)), ((), ())),
                                preferred_element_type=F32)
            if r0 == 0:
                s = jnp.where(row_ok, s, -1e30)
            s_scr[m, r0:r0 + n, :] = s
            cmax = jnp.max(fold8(s), axis=0)
            part = cmax if part is None else jnp.maximum(part, cmax)
        col_max.append(jnp.max(part, axis=0, keepdims=True))

    col_sum = []
    for m in range(2):
        part = None
        for r0, n in chunks:
            e = jnp.exp2(s_scr[m, r0:r0 + n, :] - col_max[m])
            csum = jnp.sum(fold8(e), axis=0)
            part = csum if part is None else part + csum
            pv = jnp.dot(vt_ref[:, r0:r0 + n], e.astype(BF16), preferred_element_type=F32)
            if r0 == 0:
                acc_scr[m] = pv
            else:
                acc_scr[m] += pv
        col_sum.append(jnp.sum(part, axis=0, keepdims=True))

    lam4 = lam_ref[...]
    dot1 = jnp.sum(lam4[0:1] * lam4[1:2], axis=-1, keepdims=True)
    dot2 = jnp.sum(lam4[2:3] * lam4[3:4], axis=-1, keepdims=True)
    lam = jnp.exp(dot1) - jnp.exp(dot2) + lam_init
    ot = acc_scr[0] * (1.0 / col_sum[0]) - acc_scr[1] * (lam / col_sum[1])
    o = ot.T
    o = o * lax.rsqrt(jnp.mean(o * o, axis=-1, keepdims=True) + EPS) * g_ref[...]
    o_ref[...] = (o * (1.0 - lam_init)).astype(o_ref.dtype)


def _diff_attention(qk, vt, lam4, subln_g, batch, t_pad, tq, tk, lam_init):
    m = qk.shape[0]
    hw = 2 * lam4.shape[1]
    nq = t_pad // tq
    return pl.pallas_call(
        functools.partial(_attn_kernel, lam_init=lam_init, tk=tk),
        grid=(batch, DA_HEADS, nq),
        in_specs=[pl.BlockSpec(lam4.shape, lambda b, h, i: (0, 0)),
                  pl.BlockSpec((tq, hw), lambda b, h, i: (b * nq + i, h)),
                  pl.BlockSpec((t_pad, hw), lambda b, h, i: (b, DA_HEADS + h)),
                  pl.BlockSpec((hw, t_pad), lambda b, h, i: (b * DA_HEADS + h, 0)),
                  pl.BlockSpec((1, hw), lambda b, h, i: (0, 0))],
        out_specs=pl.BlockSpec((tq, hw), lambda b, h, i: (b * nq + i, h)),
        out_shape=jax.ShapeDtypeStruct((m, DA_HEADS * hw), BF16),
        scratch_shapes=[pltpu.VMEM((2, t_pad, tq), F32), pltpu.VMEM((2, hw, tq), F32)],
        compiler_params=_cparams("parallel", "parallel", "arbitrary"),
        name="diff_attention",
    )(lam4, qk, qk, vt, subln_g.reshape(1, hw))


def _conv_kernel(prev_ref, cur_ref, next_ref, w_ref, b_ref, lg_ref, lb_ref, o_ref, z_scr, zs_scr, y_scr,
                 *, n_tiles):
    i = pl.program_id(1)
    c = o_ref.shape[1]
    tt = o_ref.shape[0]

    def glu(u):
        return u[:, :c] * jax.nn.sigmoid(u[:, c:])

    z_scr[0:CONV_HALO, :] = jnp.where(i > 0, glu(prev_ref[...]), 0.0)
    z_scr[CONV_HALO:CONV_HALO + tt, :] = glu(cur_ref[...])
    z_scr[CONV_HALO + tt:, :] = jnp.where(i < n_tiles - 1, glu(next_ref[...]), 0.0)
    n_shift = zs_scr.shape[1]
    for s in range(1, SUBLANES):
        zs_scr[s - 1] = z_scr[s:s + n_shift, :]
    half = CONV_WIDTH // 2
    rows = _largest_divisor(tt, (CONV_ROWS, SUBLANES))
    for r in range(0, tt, rows):
        for c0 in range(0, c, CONV_COLS):
            acc = b_ref[:, c0:c0 + CONV_COLS]
            for k in range(CONV_WIDTH):
                base, s = divmod(CONV_HALO - half + k, SUBLANES)
                src = z_scr if s == 0 else zs_scr.at[s - 1]
                r_src = base * SUBLANES + r
                acc = acc + src[r_src:r_src + rows, c0:c0 + CONV_COLS] * w_ref[k:k + 1, c0:c0 + CONV_COLS]
            y_scr[r:r + rows, c0:c0 + CONV_COLS] = acc
    for r in range(0, tt, rows):
        y = y_scr[r:r + rows, :]
        mu = jnp.mean(y, axis=-1, keepdims=True)
        yc = y - mu
        var = jnp.mean(yc * yc, axis=-1, keepdims=True)
        y = yc * lax.rsqrt(var + EPS) * lg_ref[...] + lb_ref[...]
        o_ref[r:r + rows, :] = (y * jax.nn.sigmoid(y)).astype(o_ref.dtype)


def _conv_module(u32, dw_w, dw_b, ln_g, ln_b, batch, t_pad, tt):
    m = u32.shape[0]
    c = dw_w.shape[1]
    nt = t_pad // tt
    hb = tt // CONV_HALO
    last_halo = m // CONV_HALO - 1
    return pl.pallas_call(
        functools.partial(_conv_kernel, n_tiles=nt),
        grid=(batch, nt),
        in_specs=[pl.BlockSpec((CONV_HALO, 2 * c), lambda b, i: (jnp.maximum((b * nt + i) * hb - 1, 0), 0)),
                  pl.BlockSpec((tt, 2 * c), lambda b, i: (b * nt + i, 0)),
                  pl.BlockSpec((CONV_HALO, 2 * c), lambda b, i: (jnp.minimum((b * nt + i + 1) * hb, last_halo), 0)),
                  pl.BlockSpec((CONV_WIDTH, c), lambda b, i: (0, 0)),
                  pl.BlockSpec((1, c), lambda b, i: (0, 0)),
                  pl.BlockSpec((1, c), lambda b, i: (0, 0)),
                  pl.BlockSpec((1, c), lambda b, i: (0, 0))],
        out_specs=pl.BlockSpec((tt, c), lambda b, i: (b * nt + i, 0)),
        out_shape=jax.ShapeDtypeStruct((m, c), BF16),
        scratch_shapes=[pltpu.VMEM((tt + 2 * CONV_HALO, c), F32),
                        pltpu.VMEM((SUBLANES - 1, tt + 2 * CONV_HALO - SUBLANES, c), F32),
                        pltpu.VMEM((tt, c), F32)],
        compiler_params=_cparams("parallel", "arbitrary"),
        name="conv_module",
    )(u32, u32, u32, dw_w, dw_b.reshape(1, c), ln_g.reshape(1, c), ln_b.reshape(1, c))


def _log_sigmoid(x):
    return jnp.minimum(x, 0.0) - jnp.log(1.0 + jnp.exp(-jnp.abs(x)))


def _gla_tables(n):
    levels = n.bit_length() - 1
    assert n == 1 << levels
    sums, pair_level = [], []
    for reverse in (False, True):
        pos = np.arange(n)[::-1] if reverse else np.arange(n)
        incl = (pos[None, :] <= pos[:, None]).astype(np.float32)
        blocks = [incl]
        for k in reversed(range(levels)):
            half = 1 << k
            boundary = (pos // (2 * half)) * (2 * half) + half - 1
            blocks.append(incl - (pos[None, :] <= boundary[:, None]).astype(np.float32))
        stacked = np.concatenate(blocks, axis=0)
        sums.append(np.concatenate([stacked, stacked], axis=1))
        differ = pos[:, None] ^ pos[None, :]
        level = np.floor(np.log2(np.maximum(differ, 1))).astype(np.int32) + 1
        level = np.where(pos[:, None] > pos[None, :], level, np.where(differ == 0, 0, -1))
        pair_level.append(np.concatenate([level, level], axis=1))
    return jnp.asarray(np.stack(sums), BF16), jnp.asarray(np.stack(pair_level), jnp.int32)


class _GlaStream:
    def __init__(self, q_ref, k_ref, v_ref, lr_ref, wg_ref, bg_ref, sums_ref, level_ref, o_ref, s_scr, a_scr,
                 reverse):
        self.q_ref, self.k_ref, self.v_ref, self.lr_ref = q_ref, k_ref, v_ref, lr_ref
        self.wg_ref, self.bg_ref, self.o_ref, self.s_scr = wg_ref, bg_ref, o_ref, s_scr
        self.sums_ref, self.level_ref, self.a_scr = sums_ref, level_ref, a_scr
        self.edge = 0 if reverse else q_ref.shape[0] - 1


def _gla_chunk(streams, q_scale):
    heads, dv, dk = streams[0].s_scr.shape
    n = streams[0].q_ref.shape[0]
    levels = n.bit_length() - 1
    nt = (((1,), (1,)), ((), ()))
    first_head = lax.broadcasted_iota(jnp.int32, (n, 2 * dk), 1) < dk
    for s in streams:
        gate = jnp.dot(s.lr_ref[...], s.wg_ref[...], preferred_element_type=F32) + s.bg_ref[...]
        la = _log_sigmoid(gate) * (1.0 / GLA_TAU)
        la_hi = la.astype(BF16)
        s.la_split = jnp.concatenate([la_hi, (la - la_hi.astype(F32)).astype(BF16)], axis=0)
    for s in streams:
        s.b = jnp.dot(s.sums_ref[0:n, :], s.la_split, preferred_element_type=F32)
    worst = None
    for s in streams:
        b = s.b
        b_edge = b[s.edge:s.edge + 1, :]
        s.q = s.q_ref[...] * q_scale
        s.k = s.k_ref[...]
        s.q_inter = (s.q * jnp.exp(b)).astype(BF16)
        s.k_state = (s.k * jnp.exp(b_edge - b)).astype(BF16)
        s.decay = jnp.exp(b_edge)
        total = jnp.max(-b_edge)
        worst = total if worst is None else jnp.maximum(worst, total)

    def pair_scores(s, ql, kl, pair):
        cs = slice(2 * pair * dk, 2 * (pair + 1) * dk)
        kp = kl[:, cs]
        k_blockdiag = jnp.concatenate([jnp.where(first_head, kp, 0), jnp.where(first_head, 0, kp)], axis=0)
        return lax.dot_general(ql[:, cs], k_blockdiag, nt, preferred_element_type=F32)

    @pl.when(worst <= GLA_SAFE_LOG_DECAY)
    def _():
        for i, s in enumerate(streams):
            mid = n // 2 if s.edge == 0 else n // 2 - 1
            b_mid = s.b[mid:mid + 1, :]
            ql = (s.q * jnp.exp(s.b - b_mid)).astype(BF16)
            kl = (s.k * jnp.exp(b_mid - s.b)).astype(BF16)
            level = s.level_ref[...]
            for pair in range(heads // 2):
                s.a_scr[i, pair] = jnp.where(level >= 0, pair_scores(s, ql, kl, pair), 0.0)

    @pl.when(worst > GLA_SAFE_LOG_DECAY)
    def _():
        for i, s in enumerate(streams):
            sums = jnp.dot(s.sums_ref[n:, :], s.la_split, preferred_element_type=F32)
            level = s.level_ref[...]
            acc = [jnp.zeros((n, 2 * n), F32) for _ in range(heads // 2)]
            for code in range(levels + 1):
                if code == 0:
                    ql, kl = s.q.astype(BF16), s.k.astype(BF16)
                else:
                    block = levels - code
                    f = jnp.exp(-jnp.abs(sums[block * n:(block + 1) * n]))
                    ql, kl = (s.q * f).astype(BF16), (s.k * f).astype(BF16)
                for pair in range(heads // 2):
                    acc[pair] = acc[pair] + jnp.where(level == code, pair_scores(s, ql, kl, pair), 0.0)
            for pair in range(heads // 2):
                s.a_scr[i, pair] = acc[pair]

    for s in streams:
        s.o_inter = [lax.dot_general(s.q_inter[:, h * dk:(h + 1) * dk], s.s_scr[h].astype(BF16), nt,
                                     preferred_element_type=F32) for h in range(heads)]
    for i, s in enumerate(streams):
        for h in range(heads):
            vs = slice(h * dv, (h + 1) * dv)
            a = s.a_scr[i, h // 2, :, (h % 2) * n:(h % 2 + 1) * n].astype(BF16)
            s.o_ref[:, vs] = s.o_inter[h] + jnp.dot(a, s.v_ref[:, vs], preferred_element_type=F32)
    for s in streams:
        for h in range(heads):
            cs = slice(h * dk, (h + 1) * dk)
            ds = lax.dot_general(s.v_ref[:, h * dv:(h + 1) * dv], s.k_state[:, cs], (((0,), (0,)), ((), ())),
                                 preferred_element_type=F32)
            s.s_scr[h] = s.s_scr[h] * s.decay[:, cs] + ds


def _gla_kernel(qf, kf, vf, lrf, qb, kb, vb, lrb, wgf, bgf, wgb, bgb, sums_ref, level_ref, of_ref, ob_ref,
                sf_scr, sb_scr, a_scr, *, q_scale):
    @pl.when(pl.program_id(1) == 0)
    def _():
        sf_scr[...] = jnp.zeros_like(sf_scr)
        sb_scr[...] = jnp.zeros_like(sb_scr)

    streams = []
    for g in range(qf.shape[0]):
        streams.append(_GlaStream(qf.at[g], kf.at[g], vf.at[g], lrf.at[g], wgf, bgf, sums_ref.at[0], level_ref.at[0],
                                  of_ref.at[g], sf_scr.at[g], a_scr, False))
        streams.append(_GlaStream(qb.at[g], kb.at[g], vb.at[g], lrb.at[g], wgb, bgb, sums_ref.at[1], level_ref.at[1],
                                  ob_ref.at[g], sb_scr.at[g], a_scr, True))
    _gla_chunk(streams, q_scale)


def _gla_scans(u32, ubf, ulr, wg_f, bg_f, wg_b, bg_b, batch, t_pad, kw, vw, col_q, col_k, col_v):
    m = u32.shape[0]
    nc = t_pad // GLA_BLOCK
    dk = kw // GLA_HEADS
    dv = vw // GLA_HEADS
    group = 2 if batch % 2 == 0 else 1

    def per_batch(x):
        return x.reshape(batch, t_pad, x.shape[1])

    def fwd(col):
        return lambda b, c: (b, c, col)

    def bwd(col):
        return lambda b, c: (b, nc - 1 - c, col)

    def direction_specs(idx):
        return [pl.BlockSpec((group, GLA_BLOCK, kw), idx(col_q)),
                pl.BlockSpec((group, GLA_BLOCK, kw), idx(col_k)),
                pl.BlockSpec((group, GLA_BLOCK, vw), idx(col_v)),
                pl.BlockSpec((group, GLA_BLOCK, LANES), idx(0))]

    const = lambda b, c: (0, 0)
    gate_specs = [pl.BlockSpec((LANES, kw), const), pl.BlockSpec((1, kw), const),
                  pl.BlockSpec((LANES, kw), const), pl.BlockSpec((1, kw), const)]
    u32_3, ubf_3, ulr_3 = per_batch(u32), per_batch(ubf), per_batch(ulr)
    state = pltpu.VMEM((group, GLA_HEADS, dv, dk), F32)
    assert dk == LANES and GLA_HEADS % 2 == 0
    sums, pair_level = _gla_tables(GLA_BLOCK)
    table_specs = [pl.BlockSpec(sums.shape, lambda b, c: (0, 0, 0)),
                   pl.BlockSpec(pair_level.shape, lambda b, c: (0, 0, 0))]
    o_f, o_b = pl.pallas_call(
        functools.partial(_gla_kernel, q_scale=dk ** -0.5),
        grid=(batch // group, nc),
        in_specs=direction_specs(fwd) + direction_specs(bwd) + gate_specs + table_specs,
        out_specs=[pl.BlockSpec((group, GLA_BLOCK, vw), fwd(0)), pl.BlockSpec((group, GLA_BLOCK, vw), bwd(0))],
        out_shape=[jax.ShapeDtypeStruct((batch, t_pad, vw), F32), jax.ShapeDtypeStruct((batch, t_pad, vw), F32)],
        scratch_shapes=[state, state,
                        pltpu.VMEM((2 * group, GLA_HEADS // 2, GLA_BLOCK, 2 * GLA_BLOCK), F32)],
        compiler_params=_cparams("parallel", "arbitrary"),
        name="gla_scans",
    )(u32_3, u32_3, ubf_3, ulr_3, u32_3, u32_3, ubf_3, ulr_3, wg_f, bg_f.reshape(1, kw), wg_b, bg_b.reshape(1, kw),
      sums, pair_level)
    return o_f.reshape(m, vw), o_b.reshape(m, vw)


def _gla_post_kernel(of_ref, ob_ref, r_ref, g_ref, y_ref):
    dv = g_ref.shape[1]
    for h in range(y_ref.shape[1] // dv):
        vs = slice(h * dv, (h + 1) * dv)
        o = of_ref[:, vs] + ob_ref[:, vs]
        o = o * lax.rsqrt(jnp.mean(o * o, axis=-1, keepdims=True) + EPS) * g_ref[...]
        r = r_ref[:, vs].astype(F32)
        y_ref[:, vs] = (o * (r * jax.nn.sigmoid(r))).astype(y_ref.dtype)


def _gla_post(o_f, o_b, ubf, norm_g, tm, col_r):
    m, vw = o_f.shape
    dv = norm_g.shape[0]
    return pl.pallas_call(
        _gla_post_kernel,
        grid=(m // tm,),
        in_specs=[pl.BlockSpec((tm, vw), lambda i: (i, 0)),
                  pl.BlockSpec((tm, vw), lambda i: (i, 0)),
                  pl.BlockSpec((tm, vw), lambda i: (i, col_r)),
                  pl.BlockSpec((1, dv), lambda i: (0, 0))],
        out_specs=pl.BlockSpec((tm, vw), lambda i: (i, 0)),
        out_shape=jax.ShapeDtypeStruct((m, vw), BF16),
        compiler_params=_cparams("parallel"),
        name="gla_post",
    )(o_f, o_b, ubf, norm_g.reshape(1, dv))


def _merge_kernel(xa_ref, xc_ref, xg_ref, wa_ref, wc_ref, wg_ref, bc_ref, ga_ref, gc_ref, gg_ref, o_ref):
    ya = jnp.dot(xa_ref[...], wa_ref[...], preferred_element_type=F32)
    yc = jnp.dot(xc_ref[...], wc_ref[...], preferred_element_type=F32) + bc_ref[...]
    yg = jnp.dot(xg_ref[...], wg_ref[...], preferred_element_type=F32)
    merged = (jax.nn.sigmoid(ga_ref[...].astype(F32)) * ya + jax.nn.sigmoid(gc_ref[...].astype(F32)) * yc
              + jax.nn.sigmoid(gg_ref[...].astype(F32)) * yg)
    o_ref[...] = merged.astype(o_ref.dtype)


def _merge(x_da, x_cv, x_gla, w_da, w_cv, w_gla, b_cv, ubf, tm, gate_col0):
    m, kdim = x_da.shape
    d = w_da.shape[1]
    x_spec = pl.BlockSpec((tm, kdim), lambda i: (i, 0))
    w_spec = pl.BlockSpec((kdim, d), lambda i: (0, 0), pipeline_mode=pl.Buffered(1))

    def gate_spec(branch):
        return pl.BlockSpec((tm, d), lambda i: (i, gate_col0 // d + branch))

    return pl.pallas_call(
        _merge_kernel,
        grid=(m // tm,),
        in_specs=[x_spec, x_spec, x_spec, w_spec, w_spec, w_spec,
                  pl.BlockSpec((1, d), lambda i: (0, 0)),
                  gate_spec(0), gate_spec(1), gate_spec(2)],
        out_specs=pl.BlockSpec((tm, d), lambda i: (i, 0)),
        out_shape=jax.ShapeDtypeStruct((m, d), BF16),
        compiler_params=_cparams("parallel"),
        name="branch_merge",
    )(x_da, x_cv, x_gla, w_da, w_cv, w_gla, b_cv.reshape(1, d), ubf, ubf, ubf)


def _out_proj_kernel(x_ref, w_ref, h_ref, o_ref, *, tiles_per_batch):
    y = jnp.dot(x_ref[...], w_ref[...], preferred_element_type=F32)
    row = lax.broadcasted_iota(jnp.int32, y.shape, 0)
    first = pl.program_id(0) % tiles_per_batch == 0
    y = jnp.where(jnp.logical_and(first, row < PAD_FRONT), 0.0, y)
    o_ref[...] = h_ref[...] + y


def _out_proj(x, w, h, tm, tiles_per_batch):
    m, kdim = x.shape
    d = w.shape[1]
    return pl.pallas_call(
        functools.partial(_out_proj_kernel, tiles_per_batch=tiles_per_batch),
        grid=(m // tm,),
        in_specs=[pl.BlockSpec((tm, kdim), lambda i: (i, 0)),
                  pl.BlockSpec((kdim, d), lambda i: (0, 0), pipeline_mode=pl.Buffered(1)),
                  pl.BlockSpec((tm, d), lambda i: (i, 0))],
        out_specs=pl.BlockSpec((tm, d), lambda i: (i, 0)),
        out_shape=jax.ShapeDtypeStruct((m, d), F32),
        compiler_params=_cparams("parallel"),
        name="out_proj",
    )(x, w, h)


def _mlp_kernel(h_ref, g_ref, w1_ref, w2_ref, gn_ref, *rest, final):
    if final:
        o_ref, a_scr = rest
    else:
        o_ref, an_ref, a_scr = rest

    @pl.when(pl.program_id(1) == 0)
    def _():
        x = h_ref[...]
        a_scr[...] = _rms(x, g_ref[...]).astype(a_scr.dtype)
        o_ref[...] = x

    t = jnp.maximum(jnp.dot(a_scr[...], w1_ref[...], preferred_element_type=F32), 0.0)
    o_ref[...] += jnp.dot((t * t).astype(BF16), w2_ref[...], preferred_element_type=F32)

    @pl.when(pl.program_id(1) == pl.num_programs(1) - 1)
    def _():
        y = _rms(o_ref[...], gn_ref[...])
        if final:
            o_ref[...] = y
        else:
            an_ref[...] = y.astype(an_ref.dtype)


def _mlp(h, g, w1, w2, g_next, tm, tf, final):
    m, d = h.shape
    ff = w1.shape[1]
    row_spec = pl.BlockSpec((tm, d), lambda i, f: (i, 0))
    vec_spec = pl.BlockSpec((1, d), lambda i, f: (0, 0))
    out_specs = row_spec if final else [row_spec, row_spec]
    out_shape = (jax.ShapeDtypeStruct((m, d), F32) if final
                 else [jax.ShapeDtypeStruct((m, d), F32), jax.ShapeDtypeStruct((m, d), BF16)])
    return pl.pallas_call(
        functools.partial(_mlp_kernel, final=final),
        grid=(m // tm, ff // tf),
        in_specs=[row_spec, vec_spec,
                  pl.BlockSpec((d, tf), lambda i, f: (0, f)),
                  pl.BlockSpec((tf, d), lambda i, f: (f, 0)),
                  vec_spec],
        out_specs=out_specs,
        out_shape=out_shape,
        scratch_shapes=[pltpu.VMEM((tm, d), BF16)],
        compiler_params=_cparams("parallel", "arbitrary"),
        name="mlp",
    )(h, g.reshape(1, d), w1, w2, g_next.reshape(1, d))


def kernel(x, meta_tokens, mix_norm_g, w_in, da_lambda, da_subln_g, w_da_proj, conv_dw_w, conv_dw_b, conv_ln_g, conv_ln_b, w_conv_proj, b_conv_proj, gla_gate_w_fwd, gla_gate_b_fwd, gla_gate_w_bwd, gla_gate_b_bwd, gla_norm_g, w_gla_proj, w_out, mlp_norm_g, w_mlp_in, w_mlp_out, final_norm_g):
    batch, seq, d = x.shape
    depth = w_in.shape[0]
    assert seq % LANES == 0 and meta_tokens.shape[0] == N_META
    t_pad = PAD_FRONT + N_META + seq
    m = batch * t_pad

    da_hd = da_lambda.shape[-1]
    da_w = w_da_proj.shape[1]
    da_qk = DA_HEADS * 2 * da_hd
    conv_c = conv_dw_w.shape[-1]
    gla_kw = gla_gate_w_fwd.shape[-1]
    gla_vw = w_gla_proj.shape[1]
    d_ff = w_mlp_in.shape[-1]
    assert da_hd == LANES and da_w == DA_HEADS * 2 * da_hd

    sizes = (da_qk, da_qk, da_w, 2 * conv_c, gla_kw, gla_kw, gla_vw, 2 * GLA_GATE_RANK, gla_vw, N_BRANCH * d)
    assert sum(sizes) == w_in.shape[-1]
    off = [0]
    for s in sizes:
        off.append(off[-1] + s)
    c_dq, c_dk, c_dv, c_cu, c_gq, c_gk, c_gv, c_lr, c_gr, c_gt = off[:-1]

    tm = _row_tile(t_pad, 1152)
    tm_mlp = _row_tile(t_pad, 768)
    tm_mix = _row_tile(t_pad, 576)
    tiles_per_batch = t_pad // tm
    tq = _largest_divisor(t_pad, (384, 256, 128))
    tk_attn = _largest_divisor(seq, (1024, 512, 256, 128))
    tt_conv = _largest_divisor(t_pad, (384, 256, 128))
    tm_proj = 2 * tm if tiles_per_batch % 2 == 0 else tm
    tn_proj = 1024
    tf = 1024

    pos = (jnp.arange(t_pad, dtype=F32) - PAD_FRONT)[:, None]
    inv_freq = 1.0 / (ROPE_THETA ** (jnp.arange(0, da_hd, 2, dtype=F32) / da_hd))
    ang = pos * inv_freq[None, :]
    cos_t = jnp.concatenate([jnp.cos(ang), jnp.cos(ang)], axis=-1)
    sin_t = jnp.concatenate([-jnp.sin(ang), jnp.sin(ang)], axis=-1)

    h, a = _embed(x.astype(F32), meta_tokens.astype(F32), mix_norm_g[0])

    assert c_dq == 0 and c_dk == da_qk and da_w == gla_vw == tn_proj
    assert c_cu % tn_proj == 0 and (c_gq - c_cu) % gla_kw == 0 and c_gk + gla_kw - c_cu == 3 * tn_proj
    assert c_dv % tn_proj == 0 and c_gv % tn_proj == 0 and c_lr + 2 * GLA_GATE_RANK == c_gr
    tail_tiles = (gla_vw + N_BRANCH * d) // tn_proj

    for l in range(depth):
        lam_init = 0.8 - 0.6 * math.exp(-0.3 * l)
        wl = w_in[l].astype(BF16)
        w_tail = wl[:, c_gr:]
        w_lr = jnp.concatenate([wl[:, c_lr:c_gr], jnp.zeros((d, LANES - 2 * GLA_GATE_RANK), BF16)], axis=1)

        qk = _rope_proj(a, wl, 2 * da_qk, cos_t, sin_t, tm, tiles_per_batch, da_hd ** -0.5 * math.log2(math.e))
        ulr = _proj(a, w_lr, tm_proj, LANES, BF16, "proj_lr", 1, lambda j: 0)
        u32 = _proj(a, wl, tm_proj, tn_proj, F32, "proj_f32", 3, lambda j: c_cu // tn_proj + j)
        uv = _proj(a, wl, tm_proj, tn_proj, BF16, "proj_v", 2,
                   lambda j: c_dv // tn_proj + j * ((c_gv - c_dv) // tn_proj))
        ug = _proj(a, w_tail, tm_proj, tn_proj, BF16, "proj_gates", tail_tiles, lambda j: (j + 1) % tail_tiles)
        f_gq, f_gk = c_gq - c_cu, c_gk - c_cu

        vt = uv[:, :da_w].reshape(batch, t_pad, da_w).transpose(0, 2, 1).reshape(batch * da_w, t_pad)
        x_da = _diff_attention(qk, vt, da_lambda[l].astype(F32), da_subln_g[l], batch, t_pad, tq, tk_attn, lam_init)
        x_cv = _conv_module(u32, conv_dw_w[l], conv_dw_b[l], conv_ln_g[l], conv_ln_b[l], batch, t_pad, tt_conv)

        gate_rows = jnp.zeros((LANES, gla_kw), F32)
        wg_f = gate_rows.at[:GLA_GATE_RANK].set(gla_gate_w_fwd[l]).astype(BF16)
        wg_b = gate_rows.at[GLA_GATE_RANK:2 * GLA_GATE_RANK].set(gla_gate_w_bwd[l]).astype(BF16)
        o_f, o_b = _gla_scans(u32, uv, ulr, wg_f, gla_gate_b_fwd[l], wg_b, gla_gate_b_bwd[l], batch, t_pad,
                              gla_kw, gla_vw, f_gq // gla_kw, f_gk // gla_kw, 1)
        x_gla = _gla_post(o_f, o_b, ug, gla_norm_g[l], tm, N_BRANCH * d // gla_vw)

        merged = _merge(x_da, x_cv, x_gla, w_da_proj[l].astype(BF16), w_conv_proj[l].astype(BF16),
                        w_gla_proj[l].astype(BF16), b_conv_proj[l], ug, tm_mix, 0)
        h = _out_proj(merged, w_out[l].astype(BF16), h, tm_mix, t_pad // tm_mix)
        w1, w2 = w_mlp_in[l].astype(BF16), w_mlp_out[l].astype(BF16)
        if l + 1 < depth:
            h, a = _mlp(h, mlp_norm_g[l], w1, w2, mix_norm_g[l + 1], tm_mlp, tf, False)
        else:
            out = _mlp(h, mlp_norm_g[l], w1, w2, final_norm_g, tm_mlp, tf, True)

    return out.reshape(batch, t_pad, d)[:, PAD_FRONT + N_META:, :]
```

```python
import functools
import math

import jax
import jax.numpy as jnp
import numpy as np
from jax import lax
from jax.experimental import pallas as pl
from jax.experimental.pallas import tpu as pltpu

F32 = jnp.float32
BF16 = jnp.bfloat16

EPS = 1e-6
ROPE_THETA = 10000.0
N_META = 16
N_BRANCH = 3
DA_HEADS = 4
GLA_HEADS = 4
GLA_GATE_RANK = 16
GLA_TAU = 16.0
CONV_WIDTH = 31

LANES = 128
SUBLANES = 8
PAD_FRONT = LANES - N_META
GLA_BLOCK = 128
GLA_SAFE_LOG_DECAY = 80.0
CONV_HALO = 16
CONV_ROWS, CONV_COLS = 64, 256
VMEM_LIMIT = 56 * 1024 * 1024


def _cparams(*sem):
    return pltpu.CompilerParams(dimension_semantics=sem, vmem_limit_bytes=VMEM_LIMIT)


def _row_tile(t_pad, max_rows):
    for n in range(1, t_pad // 16 + 1):
        if t_pad % n == 0 and (t_pad // n) % 16 == 0 and t_pad // n <= max_rows:
            return t_pad // n
    raise ValueError(f"no row tile for padded sequence {t_pad}")


def _weight_spec(w, layer, block, index, **kwargs):
    if w.ndim == 2:
        return pl.BlockSpec(block, index, **kwargs)
    return pl.BlockSpec((None,) + tuple(block), lambda *g: (layer,) + tuple(index(*g)), **kwargs)


def _largest_divisor(n, candidates):
    for c in candidates:
        if n % c == 0:
            return c
    raise ValueError(f"no tile among {candidates} divides {n}")


def _rms(x, g):
    return x * lax.rsqrt(jnp.mean(x * x, axis=-1, keepdims=True) + EPS) * g


def _embed_kernel(x_ref, meta_ref, g_ref, h_ref, a_ref):
    @pl.when(pl.program_id(1) == 0)
    def _():
        h_ref[0, 0:PAD_FRONT, :] = jnp.zeros((PAD_FRONT, h_ref.shape[2]), h_ref.dtype)
        h_ref[0, PAD_FRONT:, :] = meta_ref[...]

    @pl.when(pl.program_id(1) > 0)
    def _():
        h_ref[0] = x_ref[0]

    a_ref[0] = _rms(h_ref[0], g_ref[...]).astype(a_ref.dtype)


def _embed(x, meta_tokens, g):
    batch, seq, d = x.shape
    t_pad = LANES + seq
    row_spec = pl.BlockSpec((1, LANES, d), lambda b, i: (b, i, 0))
    h, a = pl.pallas_call(
        _embed_kernel,
        grid=(batch, t_pad // LANES),
        in_specs=[pl.BlockSpec((1, LANES, d), lambda b, i: (b, jnp.maximum(i - 1, 0), 0)),
                  pl.BlockSpec((N_META, d), lambda b, i: (0, 0)),
                  pl.BlockSpec((1, d), lambda b, i: (0, 0))],
        out_specs=[row_spec, row_spec],
        out_shape=[jax.ShapeDtypeStruct((batch, t_pad, d), F32), jax.ShapeDtypeStruct((batch, t_pad, d), BF16)],
        compiler_params=_cparams("parallel", "arbitrary"),
        name="embed",
    )(x, meta_tokens, g.reshape(1, d))
    return h.reshape(batch * t_pad, d), a.reshape(batch * t_pad, d)


def _proj_kernel(a_ref, w_ref, o_ref):
    o_ref[...] = jnp.dot(a_ref[...], w_ref[...], preferred_element_type=F32).astype(o_ref.dtype)


def _proj(a, w, layer, tm, tn, out_dtype, name, n_tiles, col_block):
    m, k = a.shape
    return pl.pallas_call(
        _proj_kernel,
        grid=(m // tm, n_tiles),
        in_specs=[pl.BlockSpec((tm, k), lambda i, j: (i, 0)),
                  _weight_spec(w, layer, (k, tn), lambda i, j: (0, col_block(j)))],
        out_specs=pl.BlockSpec((tm, tn), lambda i, j: (i, j)),
        out_shape=jax.ShapeDtypeStruct((m, n_tiles * tn), out_dtype),
        compiler_params=_cparams("parallel", "arbitrary"),
        name=name,
    )(a, w)


def _rope_proj_kernel(a_ref, w_ref, cos_ref, sin_ref, o_ref, *, q_scale):
    acc = jnp.dot(a_ref[...], w_ref[...], preferred_element_type=F32)
    cos = cos_ref[...]
    sin = sin_ref[...]
    groups = acc.shape[1] // LANES
    for g in range(groups):
        x = acc[:, g * LANES:(g + 1) * LANES]
        swapped = pltpu.roll(x, LANES // 2, axis=1)
        y = x * cos + swapped * sin
        if g < groups // 2:
            y = y * q_scale
        o_ref[:, g * LANES:(g + 1) * LANES] = y.astype(o_ref.dtype)


def _rope_proj(a, w, layer, n, cos, sin, tm, tiles_per_batch, q_scale):
    m, k = a.shape
    return pl.pallas_call(
        functools.partial(_rope_proj_kernel, q_scale=q_scale),
        grid=(m // tm,),
        in_specs=[pl.BlockSpec((tm, k), lambda i: (i, 0)),
                  _weight_spec(w, layer, (k, n), lambda i: (0, 0), pipeline_mode=pl.Buffered(1)),
                  pl.BlockSpec((tm, LANES), lambda i: (i % tiles_per_batch, 0)),
                  pl.BlockSpec((tm, LANES), lambda i: (i % tiles_per_batch, 0))],
        out_specs=pl.BlockSpec((tm, n), lambda i: (i, 0)),
        out_shape=jax.ShapeDtypeStruct((m, n), BF16),
        compiler_params=_cparams("parallel"),
        name="rope_proj",
    )(a, w, cos, sin)


def _attn_kernel(lam_ref, q_ref, k_ref, vt_ref, g_ref, o_ref, s_scr, acc_scr, *, lam_init, tk):
    tq = q_ref.shape[0]
    d = q_ref.shape[1] // 2
    t_pad = k_ref.shape[0]
    chunks = [(0, LANES)] + [(r, tk) for r in range(LANES, t_pad, tk)]

    def fold8(x):
        return x.reshape(x.shape[0] // 8, 8, tq)

    row_ok = lax.broadcasted_iota(jnp.int32, (LANES, tq), 0) >= PAD_FRONT
    col_max = []
    for m in range(2):
        q = q_ref[:, m * d:(m + 1) * d]
        part = None
        for r0, n in chunks:
            s = lax.dot_general(k_ref[r0:r0 + n, m * d:(m + 1) * d], q, (((1,), (1,)), ((), ())),
                                preferred_element_type=F32)
            if r0 == 0:
                s = jnp.where(row_ok, s, -1e30)
            s_scr[m, r0:r0 + n, :] = s
            cmax = jnp.max(fold8(s), axis=0)
            part = cmax if part is None else jnp.maximum(part, cmax)
        col_max.append(jnp.max(part, axis=0, keepdims=True))

    col_sum = []
    for m in range(2):
        part = None
        for r0, n in chunks:
            e = jnp.exp2(s_scr[m, r0:r0 + n, :] - col_max[m])
            csum = jnp.sum(fold8(e), axis=0)
            part = csum if part is None else part + csum
            pv = jnp.dot(vt_ref[:, r0:r0 + n], e.astype(BF16), preferred_element_type=F32)
            if r0 == 0:
                acc_scr[m] = pv
            else:
                acc_scr[m] += pv
        col_sum.append(jnp.sum(part, axis=0, keepdims=True))

    lam4 = lam_ref[...]
    dot1 = jnp.sum(lam4[0:1] * lam4[1:2], axis=-1, keepdims=True)
    dot2 = jnp.sum(lam4[2:3] * lam4[3:4], axis=-1, keepdims=True)
    lam = jnp.exp(dot1) - jnp.exp(dot2) + lam_init
    ot = acc_scr[0] * (1.0 / col_sum[0]) - acc_scr[1] * (lam / col_sum[1])
    o = ot.T
    o = o * lax.rsqrt(jnp.mean(o * o, axis=-1, keepdims=True) + EPS) * g_ref[...]
    o_ref[...] = (o * (1.0 - lam_init)).astype(o_ref.dtype)


def _diff_attention(qk, vt, lam4, subln_g, batch, t_pad, tq, tk, lam_init):
    m = qk.shape[0]
    hw = 2 * lam4.shape[1]
    nq = t_pad // tq
    return pl.pallas_call(
        functools.partial(_attn_kernel, lam_init=lam_init, tk=tk),
        grid=(batch, DA_HEADS, nq),
        in_specs=[pl.BlockSpec(lam4.shape, lambda b, h, i: (0, 0)),
                  pl.BlockSpec((tq, hw), lambda b, h, i: (b * nq + i, h)),
                  pl.BlockSpec((t_pad, hw), lambda b, h, i: (b, DA_HEADS + h)),
                  pl.BlockSpec((hw, t_pad), lambda b, h, i: (b * DA_HEADS + h, 0)),
                  pl.BlockSpec((1, hw), lambda b, h, i: (0, 0))],
        out_specs=pl.BlockSpec((tq, hw), lambda b, h, i: (b * nq + i, h)),
        out_shape=jax.ShapeDtypeStruct((m, DA_HEADS * hw), BF16),
        scratch_shapes=[pltpu.VMEM((2, t_pad, tq), F32), pltpu.VMEM((2, hw, tq), F32)],
        compiler_params=_cparams("parallel", "parallel", "arbitrary"),
        name="diff_attention",
    )(lam4, qk, qk, vt, subln_g.reshape(1, hw))


def _conv_kernel(prev_ref, cur_ref, next_ref, w_ref, b_ref, lg_ref, lb_ref, o_ref, z_scr, zs_scr, y_scr,
                 *, n_tiles):
    i = pl.program_id(1)
    c = o_ref.shape[1]
    tt = o_ref.shape[0]

    def glu(u):
        return u[:, :c] * jax.nn.sigmoid(u[:, c:])

    z_scr[0:CONV_HALO, :] = jnp.where(i > 0, glu(prev_ref[...]), 0.0)
    z_scr[CONV_HALO:CONV_HALO + tt, :] = glu(cur_ref[...])
    z_scr[CONV_HALO + tt:, :] = jnp.where(i < n_tiles - 1, glu(next_ref[...]), 0.0)
    n_shift = zs_scr.shape[1]
    for s in range(1, SUBLANES):
        zs_scr[s - 1] = z_scr[s:s + n_shift, :]
    half = CONV_WIDTH // 2
    rows = _largest_divisor(tt, (CONV_ROWS, SUBLANES))
    for r in range(0, tt, rows):
        for c0 in range(0, c, CONV_COLS):
            acc = b_ref[:, c0:c0 + CONV_COLS]
            for k in range(CONV_WIDTH):
                base, s = divmod(CONV_HALO - half + k, SUBLANES)
                src = z_scr if s == 0 else zs_scr.at[s - 1]
                r_src = base * SUBLANES + r
                acc = acc + src[r_src:r_src + rows, c0:c0 + CONV_COLS] * w_ref[k:k + 1, c0:c0 + CONV_COLS]
            y_scr[r:r + rows, c0:c0 + CONV_COLS] = acc
    for r in range(0, tt, rows):
        y = y_scr[r:r + rows, :]
        mu = jnp.mean(y, axis=-1, keepdims=True)
        yc = y - mu
        var = jnp.mean(yc * yc, axis=-1, keepdims=True)
        y = yc * lax.rsqrt(var + EPS) * lg_ref[...] + lb_ref[...]
        o_ref[r:r + rows, :] = (y * jax.nn.sigmoid(y)).astype(o_ref.dtype)


def _conv_module(u32, dw_w, dw_b, ln_g, ln_b, batch, t_pad, tt):
    m = u32.shape[0]
    c = dw_w.shape[1]
    nt = t_pad // tt
    hb = tt // CONV_HALO
    last_halo = m // CONV_HALO - 1
    return pl.pallas_call(
        functools.partial(_conv_kernel, n_tiles=nt),
        grid=(batch, nt),
        in_specs=[pl.BlockSpec((CONV_HALO, 2 * c), lambda b, i: (jnp.maximum((b * nt + i) * hb - 1, 0), 0)),
                  pl.BlockSpec((tt, 2 * c), lambda b, i: (b * nt + i, 0)),
                  pl.BlockSpec((CONV_HALO, 2 * c), lambda b, i: (jnp.minimum((b * nt + i + 1) * hb, last_halo), 0)),
                  pl.BlockSpec((CONV_WIDTH, c), lambda b, i: (0, 0)),
                  pl.BlockSpec((1, c), lambda b, i: (0, 0)),
                  pl.BlockSpec((1, c), lambda b, i: (0, 0)),
                  pl.BlockSpec((1, c), lambda b, i: (0, 0))],
        out_specs=pl.BlockSpec((tt, c), lambda b, i: (b * nt + i, 0)),
        out_shape=jax.ShapeDtypeStruct((m, c), BF16),
        scratch_shapes=[pltpu.VMEM((tt + 2 * CONV_HALO, c), F32),
                        pltpu.VMEM((SUBLANES - 1, tt + 2 * CONV_HALO - SUBLANES, c), F32),
                        pltpu.VMEM((tt, c), F32)],
        compiler_params=_cparams("parallel", "arbitrary"),
        name="conv_module",
    )(u32, u32, u32, dw_w, dw_b.reshape(1, c), ln_g.reshape(1, c), ln_b.reshape(1, c))


def _log_sigmoid(x):
    return jnp.minimum(x, 0.0) - jnp.log(1.0 + jnp.exp(-jnp.abs(x)))


def _gla_tables(n):
    levels = n.bit_length() - 1
    assert n == 1 << levels
    sums, pair_level = [], []
    for reverse in (False, True):
        pos = np.arange(n)[::-1] if reverse else np.arange(n)
        incl = (pos[None, :] <= pos[:, None]).astype(np.float32)
        blocks = [incl]
        for k in reversed(range(levels)):
            half = 1 << k
            boundary = (pos // (2 * half)) * (2 * half) + half - 1
            blocks.append(incl - (pos[None, :] <= boundary[:, None]).astype(np.float32))
        stacked = np.concatenate(blocks, axis=0)
        sums.append(np.concatenate([stacked, stacked], axis=1))
        differ = pos[:, None] ^ pos[None, :]
        level = np.floor(np.log2(np.maximum(differ, 1))).astype(np.int32) + 1
        level = np.where(pos[:, None] > pos[None, :], level, np.where(differ == 0, 0, -1))
        pair_level.append(np.concatenate([level, level], axis=1))
    return jnp.asarray(np.stack(sums), BF16), jnp.asarray(np.stack(pair_level), jnp.int32)


class _GlaStream:
    def __init__(self, q_ref, k_ref, v_ref, lr_ref, wg_ref, bg_ref, sums_ref, level_ref, o_ref, s_scr, a_scr,
                 reverse):
        self.q_ref, self.k_ref, self.v_ref, self.lr_ref = q_ref, k_ref, v_ref, lr_ref
        self.wg_ref, self.bg_ref, self.o_ref, self.s_scr = wg_ref, bg_ref, o_ref, s_scr
        self.sums_ref, self.level_ref, self.a_scr = sums_ref, level_ref, a_scr
        self.edge = 0 if reverse else q_ref.shape[0] - 1


def _gla_chunk(streams, q_scale):
    heads, dv, dk = streams[0].s_scr.shape
    n = streams[0].q_ref.shape[0]
    levels = n.bit_length() - 1
    nt = (((1,), (1,)), ((), ()))
    first_head = lax.broadcasted_iota(jnp.int32, (n, 2 * dk), 1) < dk
    for s in streams:
        gate = jnp.dot(s.lr_ref[...], s.wg_ref[...], preferred_element_type=F32) + s.bg_ref[...]
        la = _log_sigmoid(gate) * (1.0 / GLA_TAU)
        la_hi = la.astype(BF16)
        s.la_split = jnp.concatenate([la_hi, (la - la_hi.astype(F32)).astype(BF16)], axis=0)
    for s in streams:
        s.b = jnp.dot(s.sums_ref[0:n, :], s.la_split, preferred_element_type=F32)
    worst = None
    for s in streams:
        b = s.b
        b_edge = b[s.edge:s.edge + 1, :]
        s.q = s.q_ref[...] * q_scale
        s.k = s.k_ref[...]
        s.q_inter = (s.q * jnp.exp(b)).astype(BF16)
        s.k_state = (s.k * jnp.exp(b_edge - b)).astype(BF16)
        s.decay = jnp.exp(b_edge)
        total = jnp.max(-b_edge)
        worst = total if worst is None else jnp.maximum(worst, total)

    def pair_scores(s, ql, kl, pair):
        cs = slice(2 * pair * dk, 2 * (pair + 1) * dk)
        kp = kl[:, cs]
        k_blockdiag = jnp.concatenate([jnp.where(first_head, kp, 0), jnp.where(first_head, 0, kp)], axis=0)
        return lax.dot_general(ql[:, cs], k_blockdiag, nt, preferred_element_type=F32)

    @pl.when(worst <= GLA_SAFE_LOG_DECAY)
    def _():
        for i, s in enumerate(streams):
            mid = n // 2 if s.edge == 0 else n // 2 - 1
            b_mid = s.b[mid:mid + 1, :]
            ql = (s.q * jnp.exp(s.b - b_mid)).astype(BF16)
            kl = (s.k * jnp.exp(b_mid - s.b)).astype(BF16)
            level = s.level_ref[...]
            for pair in range(heads // 2):
                s.a_scr[i, pair] = jnp.where(level >= 0, pair_scores(s, ql, kl, pair), 0.0)

    @pl.when(worst > GLA_SAFE_LOG_DECAY)
    def _():
        for i, s in enumerate(streams):
            sums = jnp.dot(s.sums_ref[n:, :], s.la_split, preferred_element_type=F32)
            level = s.level_ref[...]
            acc = [jnp.zeros((n, 2 * n), F32) for _ in range(heads // 2)]
            for code in range(levels + 1):
                if code == 0:
                    ql, kl = s.q.astype(BF16), s.k.astype(BF16)
                else:
                    block = levels - code
                    f = jnp.exp(-jnp.abs(sums[block * n:(block + 1) * n]))
                    ql, kl = (s.q * f).astype(BF16), (s.k * f).astype(BF16)
                for pair in range(heads // 2):
                    acc[pair] = acc[pair] + jnp.where(level == code, pair_scores(s, ql, kl, pair), 0.0)
            for pair in range(heads // 2):
                s.a_scr[i, pair] = acc[pair]

    for s in streams:
        s.o_inter = [lax.dot_general(s.q_inter[:, h * dk:(h + 1) * dk], s.s_scr[h].astype(BF16), nt,
                                     preferred_element_type=F32) for h in range(heads)]
    for i, s in enumerate(streams):
        for h in range(heads):
            vs = slice(h * dv, (h + 1) * dv)
            a = s.a_scr[i, h // 2, :, (h % 2) * n:(h % 2 + 1) * n].astype(BF16)
            s.o_ref[:, vs] = s.o_inter[h] + jnp.dot(a, s.v_ref[:, vs], preferred_element_type=F32)
    for s in streams:
        for h in range(heads):
            cs = slice(h * dk, (h + 1) * dk)
            ds = lax.dot_general(s.v_ref[:, h * dv:(h + 1) * dv], s.k_state[:, cs], (((0,), (0,)), ((), ())),
                                 preferred_element_type=F32)
            s.s_scr[h] = s.s_scr[h] * s.decay[:, cs] + ds


def _gla_kernel(qf, kf, vf, lrf, qb, kb, vb, lrb, wgf, bgf, wgb, bgb, sums_ref, level_ref, of_ref, ob_ref,
                sf_scr, sb_scr, a_scr, *, q_scale):
    @pl.when(pl.program_id(1) == 0)
    def _():
        sf_scr[...] = jnp.zeros_like(sf_scr)
        sb_scr[...] = jnp.zeros_like(sb_scr)

    streams = []
    for g in range(qf.shape[0]):
        streams.append(_GlaStream(qf.at[g], kf.at[g], vf.at[g], lrf.at[g], wgf, bgf, sums_ref.at[0], level_ref.at[0],
                                  of_ref.at[g], sf_scr.at[g], a_scr, False))
        streams.append(_GlaStream(qb.at[g], kb.at[g], vb.at[g], lrb.at[g], wgb, bgb, sums_ref.at[1], level_ref.at[1],
                                  ob_ref.at[g], sb_scr.at[g], a_scr, True))
    _gla_chunk(streams, q_scale)


def _gla_scans(u32, ubf, ulr, wg_f, bg_f, wg_b, bg_b, batch, t_pad, kw, vw, col_q, col_k, col_v):
    m = u32.shape[0]
    nc = t_pad // GLA_BLOCK
    dk = kw // GLA_HEADS
    dv = vw // GLA_HEADS
    group = 2 if batch % 2 == 0 else 1

    def per_batch(x):
        return x.reshape(batch, t_pad, x.shape[1])

    def fwd(col):
        return lambda b, c: (b, c, col)

    def bwd(col):
        return lambda b, c: (b, nc - 1 - c, col)

    def direction_specs(idx):
        return [pl.BlockSpec((group, GLA_BLOCK, kw), idx(col_q)),
                pl.BlockSpec((group, GLA_BLOCK, kw), idx(col_k)),
                pl.BlockSpec((group, GLA_BLOCK, vw), idx(col_v)),
                pl.BlockSpec((group, GLA_BLOCK, LANES), idx(0))]

    const = lambda b, c: (0, 0)
    gate_specs = [pl.BlockSpec((LANES, kw), const), pl.BlockSpec((1, kw), const),
                  pl.BlockSpec((LANES, kw), const), pl.BlockSpec((1, kw), const)]
    u32_3, ubf_3, ulr_3 = per_batch(u32), per_batch(ubf), per_batch(ulr)
    state = pltpu.VMEM((group, GLA_HEADS, dv, dk), F32)
    assert dk == LANES and GLA_HEADS % 2 == 0
    sums, pair_level = _gla_tables(GLA_BLOCK)
    table_specs = [pl.BlockSpec(sums.shape, lambda b, c: (0, 0, 0)),
                   pl.BlockSpec(pair_level.shape, lambda b, c: (0, 0, 0))]
    o_f, o_b = pl.pallas_call(
        functools.partial(_gla_kernel, q_scale=dk ** -0.5),
        grid=(batch // group, nc),
        in_specs=direction_specs(fwd) + direction_specs(bwd) + gate_specs + table_specs,
        out_specs=[pl.BlockSpec((group, GLA_BLOCK, vw), fwd(0)), pl.BlockSpec((group, GLA_BLOCK, vw), bwd(0))],
        out_shape=[jax.ShapeDtypeStruct((batch, t_pad, vw), F32), jax.ShapeDtypeStruct((batch, t_pad, vw), F32)],
        scratch_shapes=[state, state,
                        pltpu.VMEM((2 * group, GLA_HEADS // 2, GLA_BLOCK, 2 * GLA_BLOCK), F32)],
        compiler_params=_cparams("parallel", "arbitrary"),
        name="gla_scans",
    )(u32_3, u32_3, ubf_3, ulr_3, u32_3, u32_3, ubf_3, ulr_3, wg_f, bg_f.reshape(1, kw), wg_b, bg_b.reshape(1, kw),
      sums, pair_level)
    return o_f.reshape(m, vw), o_b.reshape(m, vw)


def _gla_post_kernel(of_ref, ob_ref, r_ref, g_ref, y_ref):
    dv = g_ref.shape[1]
    for h in range(y_ref.shape[1] // dv):
        vs = slice(h * dv, (h + 1) * dv)
        o = of_ref[:, vs] + ob_ref[:, vs]
        o = o * lax.rsqrt(jnp.mean(o * o, axis=-1, keepdims=True) + EPS) * g_ref[...]
        r = r_ref[:, vs].astype(F32)
        y_ref[:, vs] = (o * (r * jax.nn.sigmoid(r))).astype(y_ref.dtype)


def _gla_post(o_f, o_b, ubf, norm_g, tm, col_r):
    m, vw = o_f.shape
    dv = norm_g.shape[0]
    return pl.pallas_call(
        _gla_post_kernel,
        grid=(m // tm,),
        in_specs=[pl.BlockSpec((tm, vw), lambda i: (i, 0)),
                  pl.BlockSpec((tm, vw), lambda i: (i, 0)),
                  pl.BlockSpec((tm, vw), lambda i: (i, col_r)),
                  pl.BlockSpec((1, dv), lambda i: (0, 0))],
        out_specs=pl.BlockSpec((tm, vw), lambda i: (i, 0)),
        out_shape=jax.ShapeDtypeStruct((m, vw), BF16),
        compiler_params=_cparams("parallel"),
        name="gla_post",
    )(o_f, o_b, ubf, norm_g.reshape(1, dv))


def _merge_kernel(xa_ref, xc_ref, xg_ref, wa_ref, wc_ref, wg_ref, bc_ref, ga_ref, gc_ref, gg_ref, o_ref):
    ya = jnp.dot(xa_ref[...], wa_ref[...], preferred_element_type=F32)
    yc = jnp.dot(xc_ref[...], wc_ref[...], preferred_element_type=F32) + bc_ref[...]
    yg = jnp.dot(xg_ref[...], wg_ref[...], preferred_element_type=F32)
    merged = (jax.nn.sigmoid(ga_ref[...].astype(F32)) * ya + jax.nn.sigmoid(gc_ref[...].astype(F32)) * yc
              + jax.nn.sigmoid(gg_ref[...].astype(F32)) * yg)
    o_ref[...] = merged.astype(o_ref.dtype)


def _merge(x_da, x_cv, x_gla, w_da, w_cv, w_gla, layer, b_cv, ubf, tm, gate_col0):
    m, kdim = x_da.shape
    d = w_da.shape[-1]
    x_spec = pl.BlockSpec((tm, kdim), lambda i: (i, 0))
    w_spec = _weight_spec(w_da, layer, (kdim, d), lambda i: (0, 0), pipeline_mode=pl.Buffered(1))

    def gate_spec(branch):
        return pl.BlockSpec((tm, d), lambda i: (i, gate_col0 // d + branch))

    return pl.pallas_call(
        _merge_kernel,
        grid=(m // tm,),
        in_specs=[x_spec, x_spec, x_spec, w_spec, w_spec, w_spec,
                  pl.BlockSpec((1, d), lambda i: (0, 0)),
                  gate_spec(0), gate_spec(1), gate_spec(2)],
        out_specs=pl.BlockSpec((tm, d), lambda i: (i, 0)),
        out_shape=jax.ShapeDtypeStruct((m, d), BF16),
        compiler_params=_cparams("parallel"),
        name="branch_merge",
    )(x_da, x_cv, x_gla, w_da, w_cv, w_gla, b_cv.reshape(1, d), ubf, ubf, ubf)


def _out_proj_kernel(x_ref, w_ref, h_ref, o_ref, *, tiles_per_batch):
    y = jnp.dot(x_ref[...], w_ref[...], preferred_element_type=F32)
    row = lax.broadcasted_iota(jnp.int32, y.shape, 0)
    first = pl.program_id(0) % tiles_per_batch == 0
    y = jnp.where(jnp.logical_and(first, row < PAD_FRONT), 0.0, y)
    o_ref[...] = h_ref[...] + y


def _out_proj(x, w, layer, h, tm, tiles_per_batch):
    m, kdim = x.shape
    d = w.shape[-1]
    return pl.pallas_call(
        functools.partial(_out_proj_kernel, tiles_per_batch=tiles_per_batch),
        grid=(m // tm,),
        in_specs=[pl.BlockSpec((tm, kdim), lambda i: (i, 0)),
                  _weight_spec(w, layer, (kdim, d), lambda i: (0, 0), pipeline_mode=pl.Buffered(1)),
                  pl.BlockSpec((tm, d), lambda i: (i, 0))],
        out_specs=pl.BlockSpec((tm, d), lambda i: (i, 0)),
        out_shape=jax.ShapeDtypeStruct((m, d), F32),
        compiler_params=_cparams("parallel"),
        name="out_proj",
    )(x, w, h)


def _mlp_kernel(h_ref, g_ref, w1_ref, w2_ref, gn_ref, *rest, final):
    if final:
        o_ref, a_scr = rest
    else:
        o_ref, an_ref, a_scr = rest

    @pl.when(pl.program_id(1) == 0)
    def _():
        x = h_ref[...]
        a_scr[...] = _rms(x, g_ref[...]).astype(a_scr.dtype)
        o_ref[...] = x

    t = jnp.maximum(jnp.dot(a_scr[...], w1_ref[...], preferred_element_type=F32), 0.0)
    o_ref[...] += jnp.dot((t * t).astype(BF16), w2_ref[...], preferred_element_type=F32)

    @pl.when(pl.program_id(1) == pl.num_programs(1) - 1)
    def _():
        y = _rms(o_ref[...], gn_ref[...])
        if final:
            o_ref[...] = y
        else:
            an_ref[...] = y.astype(an_ref.dtype)


def _mlp(h, g, w1, w2, layer, g_next, tm, tf, final, batch, seq):
    m, d = h.shape
    ff = w1.shape[-1]
    row_spec = pl.BlockSpec((tm, d), lambda i, f: (i, 0))
    vec_spec = pl.BlockSpec((1, d), lambda i, f: (0, 0))
    if final:
        t_pad, tiles = m // batch, seq // tm
        h_spec = pl.BlockSpec((pl.Element(tm), pl.Element(d)),
                              lambda i, f: (pl.multiple_of((i // tiles) * t_pad + (t_pad - seq) + (i % tiles) * tm,
                                                           SUBLANES), 0))
        n_rows, out_specs = batch * seq, row_spec
        out_shape = jax.ShapeDtypeStruct((n_rows, d), F32)
    else:
        h_spec, n_rows, out_specs = row_spec, m, [row_spec, row_spec]
        out_shape = [jax.ShapeDtypeStruct((m, d), F32), jax.ShapeDtypeStruct((m, d), BF16)]
    return pl.pallas_call(
        functools.partial(_mlp_kernel, final=final),
        grid=(n_rows // tm, ff // tf),
        in_specs=[h_spec, vec_spec,
                  _weight_spec(w1, layer, (d, tf), lambda i, f: (0, f)),
                  _weight_spec(w2, layer, (tf, d), lambda i, f: (f, 0)),
                  vec_spec],
        out_specs=out_specs,
        out_shape=out_shape,
        scratch_shapes=[pltpu.VMEM((tm, d), BF16)],
        compiler_params=_cparams("parallel", "arbitrary"),
        name="mlp",
    )(h, g.reshape(1, d), w1, w2, g_next.reshape(1, d))


def kernel(x, meta_tokens, mix_norm_g, w_in, da_lambda, da_subln_g, w_da_proj, conv_dw_w, conv_dw_b, conv_ln_g, conv_ln_b, w_conv_proj, b_conv_proj, gla_gate_w_fwd, gla_gate_b_fwd, gla_gate_w_bwd, gla_gate_b_bwd, gla_norm_g, w_gla_proj, w_out, mlp_norm_g, w_mlp_in, w_mlp_out, final_norm_g):
    batch, seq, d = x.shape
    depth = w_in.shape[0]
    assert seq % LANES == 0 and meta_tokens.shape[0] == N_META
    t_pad = PAD_FRONT + N_META + seq
    m = batch * t_pad

    da_hd = da_lambda.shape[-1]
    da_w = w_da_proj.shape[1]
    da_qk = DA_HEADS * 2 * da_hd
    conv_c = conv_dw_w.shape[-1]
    gla_kw = gla_gate_w_fwd.shape[-1]
    gla_vw = w_gla_proj.shape[1]
    d_ff = w_mlp_in.shape[-1]
    assert da_hd == LANES and da_w == DA_HEADS * 2 * da_hd

    sizes = (da_qk, da_qk, da_w, 2 * conv_c, gla_kw, gla_kw, gla_vw, 2 * GLA_GATE_RANK, gla_vw, N_BRANCH * d)
    assert sum(sizes) == w_in.shape[-1]
    off = [0]
    for s in sizes:
        off.append(off[-1] + s)
    c_dq, c_dk, c_dv, c_cu, c_gq, c_gk, c_gv, c_lr, c_gr, c_gt = off[:-1]

    tm = _row_tile(t_pad, 1152)
    tm_mlp = _row_tile(t_pad, 576)
    tm_mix = _row_tile(t_pad, 576)
    tiles_per_batch = t_pad // tm
    tq = _largest_divisor(t_pad, (384, 256, 128))
    tk_attn = _largest_divisor(seq, (1024, 512, 256, 128))
    tt_conv = _largest_divisor(t_pad, (384, 256, 128))
    tm_proj = 2 * tm if tiles_per_batch % 2 == 0 else tm
    tn_proj = 1024
    tf = 1024

    pos = (jnp.arange(t_pad, dtype=F32) - PAD_FRONT)[:, None]
    inv_freq = 1.0 / (ROPE_THETA ** (jnp.arange(0, da_hd, 2, dtype=F32) / da_hd))
    ang = pos * inv_freq[None, :]
    cos_t = jnp.concatenate([jnp.cos(ang), jnp.cos(ang)], axis=-1)
    sin_t = jnp.concatenate([-jnp.sin(ang), jnp.sin(ang)], axis=-1)

    h, a = _embed(x.astype(F32), meta_tokens.astype(F32), mix_norm_g[0])

    assert c_dq == 0 and c_dk == da_qk and da_w == gla_vw == tn_proj
    assert c_cu % tn_proj == 0 and (c_gq - c_cu) % gla_kw == 0 and c_gk + gla_kw - c_cu == 3 * tn_proj
    assert c_dv % tn_proj == 0 and c_gv % tn_proj == 0 and c_lr + 2 * GLA_GATE_RANK == c_gr
    tail_tiles = (gla_vw + N_BRANCH * d) // tn_proj

    w_in_bf, w_out_bf = w_in.astype(BF16), w_out.astype(BF16)
    w_da_bf, w_cv_bf, w_gla_bf = w_da_proj.astype(BF16), w_conv_proj.astype(BF16), w_gla_proj.astype(BF16)
    w_mlp_in_bf, w_mlp_out_bf = w_mlp_in.astype(BF16), w_mlp_out.astype(BF16)
    w_tail_bf = w_in_bf[:, :, c_gr:]
    w_lr_bf = jnp.concatenate([w_in_bf[:, :, c_lr:c_gr],
                               jnp.zeros((depth, d, LANES - 2 * GLA_GATE_RANK), BF16)], axis=2)
    tm_last = _largest_divisor(seq, (512, 256, 128))

    for l in range(depth):
        lam_init = 0.8 - 0.6 * math.exp(-0.3 * l)
        qk = _rope_proj(a, w_in_bf, l, 2 * da_qk, cos_t, sin_t, tm, tiles_per_batch,
                        da_hd ** -0.5 * math.log2(math.e))
        ulr = _proj(a, w_lr_bf, l, tm_proj, LANES, BF16, "proj_lr", 1, lambda j: 0)
        u32 = _proj(a, w_in_bf, l, tm_proj, tn_proj, F32, "proj_f32", 3, lambda j: c_cu // tn_proj + j)
        uv = _proj(a, w_in_bf, l, tm_proj, tn_proj, BF16, "proj_v", 2,
                   lambda j: c_dv // tn_proj + j * ((c_gv - c_dv) // tn_proj))
        ug = _proj(a, w_tail_bf, l, tm_proj, tn_proj, BF16, "proj_gates", tail_tiles,
                   lambda j: (j + 1) % tail_tiles)
        f_gq, f_gk = c_gq - c_cu, c_gk - c_cu

        vt = uv[:, :da_w].reshape(batch, t_pad, da_w).transpose(0, 2, 1).reshape(batch * da_w, t_pad)
        x_da = _diff_attention(qk, vt, da_lambda[l].astype(F32), da_subln_g[l], batch, t_pad, tq, tk_attn, lam_init)
        x_cv = _conv_module(u32, conv_dw_w[l], conv_dw_b[l], conv_ln_g[l], conv_ln_b[l], batch, t_pad, tt_conv)

        gate_rows = jnp.zeros((LANES, gla_kw), F32)
        wg_f = gate_rows.at[:GLA_GATE_RANK].set(gla_gate_w_fwd[l]).astype(BF16)
        wg_b = gate_rows.at[GLA_GATE_RANK:2 * GLA_GATE_RANK].set(gla_gate_w_bwd[l]).astype(BF16)
        o_f, o_b = _gla_scans(u32, uv, ulr, wg_f, gla_gate_b_fwd[l], wg_b, gla_gate_b_bwd[l], batch, t_pad,
                              gla_kw, gla_vw, f_gq // gla_kw, f_gk // gla_kw, 1)
        x_gla = _gla_post(o_f, o_b, ug, gla_norm_g[l], tm, N_BRANCH * d // gla_vw)

        merged = _merge(x_da, x_cv, x_gla, w_da_bf, w_cv_bf, w_gla_bf, l, b_conv_proj[l], ug, tm_mix, 0)
        h = _out_proj(merged, w_out_bf, l, h, tm_mix, t_pad // tm_mix)
        if l + 1 < depth:
            h, a = _mlp(h, mlp_norm_g[l], w_mlp_in_bf, w_mlp_out_bf, l, mix_norm_g[l + 1], tm_mlp, tf, False,
                        batch, seq)
        else:
            out = _mlp(h, mlp_norm_g[l], w_mlp_in_bf, w_mlp_out_bf, l, final_norm_g, tm_last, tf, True,
                       batch, seq)

    return out.reshape(batch, seq, d)
```

```python
import functools
import math

import jax
import jax.numpy as jnp
import numpy as np
from jax import lax
from jax.experimental import pallas as pl
from jax.experimental.pallas import tpu as pltpu

F32 = jnp.float32
BF16 = jnp.bfloat16

EPS = 1e-6
ROPE_THETA = 10000.0
N_META = 16
N_BRANCH = 3
DA_HEADS = 4
GLA_HEADS = 4
GLA_GATE_RANK = 16
GLA_TAU = 16.0
CONV_WIDTH = 31

LANES = 128
SUBLANES = 8
PAD_FRONT = LANES - N_META
GLA_BLOCK = 128
GLA_SAFE_LOG_DECAY = 80.0
CONV_HALO = 16
CONV_ROWS, CONV_COLS = 64, 256
VMEM_LIMIT = 56 * 1024 * 1024


def _cparams(*sem):
    return pltpu.CompilerParams(dimension_semantics=sem, vmem_limit_bytes=VMEM_LIMIT)


def _row_tile(t_pad, max_rows):
    for n in range(1, t_pad // 16 + 1):
        if t_pad % n == 0 and (t_pad // n) % 16 == 0 and t_pad // n <= max_rows:
            return t_pad // n
    raise ValueError(f"no row tile for padded sequence {t_pad}")


def _weight_spec(w, layer, block, index, **kwargs):
    if w.ndim == 2:
        return pl.BlockSpec(block, index, **kwargs)
    return pl.BlockSpec((None,) + tuple(block), lambda *g: (layer,) + tuple(index(*g)), **kwargs)


def _largest_divisor(n, candidates):
    for c in candidates:
        if n % c == 0:
            return c
    raise ValueError(f"no tile among {candidates} divides {n}")


def _rms(x, g):
    return x * lax.rsqrt(jnp.mean(x * x, axis=-1, keepdims=True) + EPS) * g


def _embed_kernel(*refs):
    *x_refs, meta_ref, g_ref, h_ref, a_ref = refs

    @pl.when(pl.program_id(1) == 0)
    def _():
        h_ref[0, 0:PAD_FRONT, :] = jnp.zeros((PAD_FRONT, h_ref.shape[2]), h_ref.dtype)
        h_ref[0, PAD_FRONT:LANES, :] = meta_ref[...]

    @pl.when(pl.program_id(1) > 0)
    def _():
        h_ref[0, 0:LANES, :] = x_refs[0][0]

    for j in range(1, len(x_refs)):
        h_ref[0, j * LANES:(j + 1) * LANES, :] = x_refs[j][0]
    a_ref[0] = _rms(h_ref[0], g_ref[...]).astype(a_ref.dtype)


def _embed(x, meta_tokens, g):
    batch, seq, d = x.shape
    blocks = (LANES + seq) // LANES
    per_step = _largest_divisor(blocks, (3, 2, 1))
    row_spec = pl.BlockSpec((1, per_step * LANES, d), lambda b, i: (b, i, 0))

    def x_spec(j):
        return pl.BlockSpec((1, LANES, d), lambda b, i: (b, jnp.maximum(i * per_step + j - 1, 0), 0))

    padded = (batch, blocks * LANES, d)
    h, a = pl.pallas_call(
        _embed_kernel,
        grid=(batch, blocks // per_step),
        in_specs=[x_spec(j) for j in range(per_step)] + [pl.BlockSpec((N_META, d), lambda b, i: (0, 0)),
                                                         pl.BlockSpec((1, d), lambda b, i: (0, 0))],
        out_specs=[row_spec, row_spec],
        out_shape=[jax.ShapeDtypeStruct(padded, F32), jax.ShapeDtypeStruct(padded, BF16)],
        compiler_params=_cparams("parallel", "arbitrary"),
        name="embed",
    )(*([x] * per_step), meta_tokens, g.reshape(1, d))
    return h.reshape(-1, d), a.reshape(-1, d)


def _proj_kernel(a_ref, w_ref, o_ref):
    o_ref[...] = jnp.dot(a_ref[...], w_ref[...], preferred_element_type=F32).astype(o_ref.dtype)


def _proj(a, w, layer, tm, tn, out_dtype, name, n_tiles, col_block):
    m, k = a.shape
    return pl.pallas_call(
        _proj_kernel,
        grid=(m // tm, n_tiles),
        in_specs=[pl.BlockSpec((tm, k), lambda i, j: (i, 0)),
                  _weight_spec(w, layer, (k, tn), lambda i, j: (0, col_block(j)))],
        out_specs=pl.BlockSpec((tm, tn), lambda i, j: (i, j)),
        out_shape=jax.ShapeDtypeStruct((m, n_tiles * tn), out_dtype),
        compiler_params=_cparams("parallel", "arbitrary"),
        name=name,
    )(a, w)


def _rope_proj_kernel(a_ref, w_ref, cos_ref, sin_ref, o_ref, *, q_scale):
    acc = jnp.dot(a_ref[...], w_ref[...], preferred_element_type=F32)
    cos = cos_ref[...]
    sin = sin_ref[...]
    groups = acc.shape[1] // LANES
    for g in range(groups):
        x = acc[:, g * LANES:(g + 1) * LANES]
        swapped = pltpu.roll(x, LANES // 2, axis=1)
        y = x * cos + swapped * sin
        if g < groups // 2:
            y = y * q_scale
        o_ref[:, g * LANES:(g + 1) * LANES] = y.astype(o_ref.dtype)


def _rope_proj(a, w, layer, n, cos, sin, tm, tiles_per_batch, q_scale):
    m, k = a.shape
    return pl.pallas_call(
        functools.partial(_rope_proj_kernel, q_scale=q_scale),
        grid=(m // tm,),
        in_specs=[pl.BlockSpec((tm, k), lambda i: (i, 0)),
                  _weight_spec(w, layer, (k, n), lambda i: (0, 0), pipeline_mode=pl.Buffered(1)),
                  pl.BlockSpec((tm, LANES), lambda i: (i % tiles_per_batch, 0)),
                  pl.BlockSpec((tm, LANES), lambda i: (i % tiles_per_batch, 0))],
        out_specs=pl.BlockSpec((tm, n), lambda i: (i, 0)),
        out_shape=jax.ShapeDtypeStruct((m, n), BF16),
        compiler_params=_cparams("parallel"),
        name="rope_proj",
    )(a, w, cos, sin)


def _attn_kernel(lam_ref, q_ref, k_ref, vt_ref, g_ref, o_ref, s_scr, acc_scr, *, lam_init, tk):
    tq = q_ref.shape[0]
    d = q_ref.shape[1] // 2
    t_pad = k_ref.shape[0]
    chunks = [(0, LANES)] + [(r, tk) for r in range(LANES, t_pad, tk)]

    def fold8(x):
        return x.reshape(x.shape[0] // 8, 8, tq)

    row_ok = lax.broadcasted_iota(jnp.int32, (LANES, tq), 0) >= PAD_FRONT
    col_max = []
    for m in range(2):
        q = q_ref[:, m * d:(m + 1) * d]
        part = None
        for r0, n in chunks:
            s = lax.dot_general(k_ref[r0:r0 + n, m * d:(m + 1) * d], q, (((1,), (1,)), ((), ())),
                                preferred_element_type=F32)
            if r0 == 0:
                s = jnp.where(row_ok, s, -1e30)
            s_scr[m, r0:r0 + n, :] = s
            cmax = jnp.max(fold8(s), axis=0)
            part = cmax if part is None else jnp.maximum(part, cmax)
        col_max.append(jnp.max(part, axis=0, keepdims=True))

    col_sum = []
    for m in range(2):
        part = None
        for r0, n in chunks:
            e = jnp.exp2(s_scr[m, r0:r0 + n, :] - col_max[m])
            csum = jnp.sum(fold8(e), axis=0)
            part = csum if part is None else part + csum
            pv = jnp.dot(vt_ref[:, r0:r0 + n], e.astype(BF16), preferred_element_type=F32)
            if r0 == 0:
                acc_scr[m] = pv
            else:
                acc_scr[m] += pv
        col_sum.append(jnp.sum(part, axis=0, keepdims=True))

    lam4 = lam_ref[...]
    dot1 = jnp.sum(lam4[0:1] * lam4[1:2], axis=-1, keepdims=True)
    dot2 = jnp.sum(lam4[2:3] * lam4[3:4], axis=-1, keepdims=True)
    lam = jnp.exp(dot1) - jnp.exp(dot2) + lam_init
    ot = acc_scr[0] * (1.0 / col_sum[0]) - acc_scr[1] * (lam / col_sum[1])
    o = ot.T
    o = o * lax.rsqrt(jnp.mean(o * o, axis=-1, keepdims=True) + EPS) * g_ref[...]
    o_ref[...] = (o * (1.0 - lam_init)).astype(o_ref.dtype)


def _diff_attention(qk, vt, lam4, subln_g, batch, t_pad, tq, tk, lam_init):
    m = qk.shape[0]
    hw = 2 * lam4.shape[1]
    nq = t_pad // tq
    return pl.pallas_call(
        functools.partial(_attn_kernel, lam_init=lam_init, tk=tk),
        grid=(batch, DA_HEADS, nq),
        in_specs=[pl.BlockSpec(lam4.shape, lambda b, h, i: (0, 0)),
                  pl.BlockSpec((tq, hw), lambda b, h, i: (b * nq + i, h)),
                  pl.BlockSpec((t_pad, hw), lambda b, h, i: (b, DA_HEADS + h)),
                  pl.BlockSpec((hw, t_pad), lambda b, h, i: (b * DA_HEADS + h, 0)),
                  pl.BlockSpec((1, hw), lambda b, h, i: (0, 0))],
        out_specs=pl.BlockSpec((tq, hw), lambda b, h, i: (b * nq + i, h)),
        out_shape=jax.ShapeDtypeStruct((m, DA_HEADS * hw), BF16),
        scratch_shapes=[pltpu.VMEM((2, t_pad, tq), F32), pltpu.VMEM((2, hw, tq), F32)],
        compiler_params=_cparams("parallel", "parallel", "arbitrary"),
        name="diff_attention",
    )(lam4, qk, qk, vt, subln_g.reshape(1, hw))


def _conv_kernel(prev_ref, cur_ref, next_ref, w_ref, b_ref, lg_ref, lb_ref, o_ref, z_scr, zs_scr, y_scr,
                 *, n_tiles):
    i = pl.program_id(1)
    c = o_ref.shape[1]
    tt = o_ref.shape[0]

    def glu(u):
        u = u.astype(F32)
        return u[:, :c] * jax.nn.sigmoid(u[:, c:])

    z_scr[0:CONV_HALO, :] = jnp.where(i > 0, glu(prev_ref[...]), 0.0)
    z_scr[CONV_HALO:CONV_HALO + tt, :] = glu(cur_ref[...])
    z_scr[CONV_HALO + tt:, :] = jnp.where(i < n_tiles - 1, glu(next_ref[...]), 0.0)
    n_shift = zs_scr.shape[1]
    for s in range(1, SUBLANES):
        zs_scr[s - 1] = z_scr[s:s + n_shift, :]
    half = CONV_WIDTH // 2
    rows = _largest_divisor(tt, (CONV_ROWS, SUBLANES))
    for r in range(0, tt, rows):
        for c0 in range(0, c, CONV_COLS):
            acc = b_ref[:, c0:c0 + CONV_COLS]
            for k in range(CONV_WIDTH):
                base, s = divmod(CONV_HALO - half + k, SUBLANES)
                src = z_scr if s == 0 else zs_scr.at[s - 1]
                r_src = base * SUBLANES + r
                acc = acc + src[r_src:r_src + rows, c0:c0 + CONV_COLS] * w_ref[k:k + 1, c0:c0 + CONV_COLS]
            y_scr[r:r + rows, c0:c0 + CONV_COLS] = acc
    for r in range(0, tt, rows):
        y = y_scr[r:r + rows, :]
        mu = jnp.mean(y, axis=-1, keepdims=True)
        yc = y - mu
        var = jnp.mean(yc * yc, axis=-1, keepdims=True)
        y = yc * lax.rsqrt(var + EPS) * lg_ref[...] + lb_ref[...]
        o_ref[r:r + rows, :] = (y * jax.nn.sigmoid(y)).astype(o_ref.dtype)


def _conv_module(u32, dw_w, dw_b, ln_g, ln_b, batch, t_pad, tt):
    m = u32.shape[0]
    c = dw_w.shape[1]
    nt = t_pad // tt
    hb = tt // CONV_HALO
    last_halo = m // CONV_HALO - 1
    return pl.pallas_call(
        functools.partial(_conv_kernel, n_tiles=nt),
        grid=(batch, nt),
        in_specs=[pl.BlockSpec((CONV_HALO, 2 * c), lambda b, i: (jnp.maximum((b * nt + i) * hb - 1, 0), 0)),
                  pl.BlockSpec((tt, 2 * c), lambda b, i: (b * nt + i, 0)),
                  pl.BlockSpec((CONV_HALO, 2 * c), lambda b, i: (jnp.minimum((b * nt + i + 1) * hb, last_halo), 0)),
                  pl.BlockSpec((CONV_WIDTH, c), lambda b, i: (0, 0)),
                  pl.BlockSpec((1, c), lambda b, i: (0, 0)),
                  pl.BlockSpec((1, c), lambda b, i: (0, 0)),
                  pl.BlockSpec((1, c), lambda b, i: (0, 0))],
        out_specs=pl.BlockSpec((tt, c), lambda b, i: (b * nt + i, 0)),
        out_shape=jax.ShapeDtypeStruct((m, c), BF16),
        scratch_shapes=[pltpu.VMEM((tt + 2 * CONV_HALO, c), F32),
                        pltpu.VMEM((SUBLANES - 1, tt + 2 * CONV_HALO - SUBLANES, c), F32),
                        pltpu.VMEM((tt, c), F32)],
        compiler_params=_cparams("parallel", "arbitrary"),
        name="conv_module",
    )(u32, u32, u32, dw_w, dw_b.reshape(1, c), ln_g.reshape(1, c), ln_b.reshape(1, c))


def _log_sigmoid(x):
    return jnp.minimum(x, 0.0) - jnp.log(1.0 + jnp.exp(-jnp.abs(x)))


def _gla_tables(n):
    levels = n.bit_length() - 1
    assert n == 1 << levels
    sums, pair_level = [], []
    for reverse in (False, True):
        pos = np.arange(n)[::-1] if reverse else np.arange(n)
        incl = (pos[None, :] <= pos[:, None]).astype(np.float32)
        blocks = [incl]
        for k in reversed(range(levels)):
            half = 1 << k
            boundary = (pos // (2 * half)) * (2 * half) + half - 1
            blocks.append(incl - (pos[None, :] <= boundary[:, None]).astype(np.float32))
        stacked = np.concatenate(blocks, axis=0)
        sums.append(np.concatenate([stacked, stacked], axis=1))
        differ = pos[:, None] ^ pos[None, :]
        level = np.floor(np.log2(np.maximum(differ, 1))).astype(np.int32) + 1
        level = np.where(pos[:, None] > pos[None, :], level, np.where(differ == 0, 0, -1))
        pair_level.append(np.concatenate([level, level], axis=1))
    return jnp.asarray(np.stack(sums), BF16), jnp.asarray(np.stack(pair_level), jnp.int32)


class _GlaStream:
    def __init__(self, q_ref, k_ref, v_ref, lr_ref, wg_ref, bg_ref, sums_ref, level_ref, o_ref, s_scr, a_scr,
                 reverse):
        self.q_ref, self.k_ref, self.v_ref, self.lr_ref = q_ref, k_ref, v_ref, lr_ref
        self.wg_ref, self.bg_ref, self.o_ref, self.s_scr = wg_ref, bg_ref, o_ref, s_scr
        self.sums_ref, self.level_ref, self.a_scr = sums_ref, level_ref, a_scr
        self.edge = 0 if reverse else q_ref.shape[0] - 1


def _gla_chunk(streams, q_scale):
    heads, dv, dk = streams[0].s_scr.shape
    n = streams[0].q_ref.shape[0]
    levels = n.bit_length() - 1
    nt = (((1,), (1,)), ((), ()))
    first_head = lax.broadcasted_iota(jnp.int32, (n, 2 * dk), 1) < dk
    for s in streams:
        gate = jnp.dot(s.lr_ref[...], s.wg_ref[...], preferred_element_type=F32) + s.bg_ref[...]
        la = _log_sigmoid(gate) * (1.0 / GLA_TAU)
        la_hi = la.astype(BF16)
        s.la_split = jnp.concatenate([la_hi, (la - la_hi.astype(F32)).astype(BF16)], axis=0)
    for s in streams:
        s.b = jnp.dot(s.sums_ref[0:n, :], s.la_split, preferred_element_type=F32)
    worst = None
    for s in streams:
        b = s.b
        b_edge = b[s.edge:s.edge + 1, :]
        s.q = s.q_ref[...].astype(F32) * q_scale
        s.k = s.k_ref[...].astype(F32)
        s.q_inter = (s.q * jnp.exp(b)).astype(BF16)
        s.k_state = (s.k * jnp.exp(b_edge - b)).astype(BF16)
        s.decay = jnp.exp(b_edge)
        total = jnp.max(-b_edge)
        worst = total if worst is None else jnp.maximum(worst, total)

    def pair_scores(s, ql, kl, pair):
        cs = slice(2 * pair * dk, 2 * (pair + 1) * dk)
        kp = kl[:, cs]
        k_blockdiag = jnp.concatenate([jnp.where(first_head, kp, 0), jnp.where(first_head, 0, kp)], axis=0)
        return lax.dot_general(ql[:, cs], k_blockdiag, nt, preferred_element_type=F32)

    @pl.when(worst <= GLA_SAFE_LOG_DECAY)
    def _():
        for i, s in enumerate(streams):
            mid = n // 2 if s.edge == 0 else n // 2 - 1
            b_mid = s.b[mid:mid + 1, :]
            ql = (s.q * jnp.exp(s.b - b_mid)).astype(BF16)
            kl = (s.k * jnp.exp(b_mid - s.b)).astype(BF16)
            level = s.level_ref[...]
            for pair in range(heads // 2):
                s.a_scr[i, pair] = jnp.where(level >= 0, pair_scores(s, ql, kl, pair), 0.0)

    @pl.when(worst > GLA_SAFE_LOG_DECAY)
    def _():
        for i, s in enumerate(streams):
            sums = jnp.dot(s.sums_ref[n:, :], s.la_split, preferred_element_type=F32)
            level = s.level_ref[...]
            acc = [jnp.zeros((n, 2 * n), F32) for _ in range(heads // 2)]
            for code in range(levels + 1):
                if code == 0:
                    ql, kl = s.q.astype(BF16), s.k.astype(BF16)
                else:
                    block = levels - code
                    f = jnp.exp(-jnp.abs(sums[block * n:(block + 1) * n]))
                    ql, kl = (s.q * f).astype(BF16), (s.k * f).astype(BF16)
                for pair in range(heads // 2):
                    acc[pair] = acc[pair] + jnp.where(level == code, pair_scores(s, ql, kl, pair), 0.0)
            for pair in range(heads // 2):
                s.a_scr[i, pair] = acc[pair]

    for s in streams:
        s.o_inter = [lax.dot_general(s.q_inter[:, h * dk:(h + 1) * dk], s.s_scr[h].astype(BF16), nt,
                                     preferred_element_type=F32) for h in range(heads)]
    for i, s in enumerate(streams):
        for h in range(heads):
            vs = slice(h * dv, (h + 1) * dv)
            a = s.a_scr[i, h // 2, :, (h % 2) * n:(h % 2 + 1) * n].astype(BF16)
            s.o_ref[:, vs] = s.o_inter[h] + jnp.dot(a, s.v_ref[:, vs], preferred_element_type=F32)
    for s in streams:
        for h in range(heads):
            cs = slice(h * dk, (h + 1) * dk)
            ds = lax.dot_general(s.v_ref[:, h * dv:(h + 1) * dv], s.k_state[:, cs], (((0,), (0,)), ((), ())),
                                 preferred_element_type=F32)
            s.s_scr[h] = s.s_scr[h] * s.decay[:, cs] + ds


def _gla_kernel(qf, kf, vf, lrf, qb, kb, vb, lrb, wgf, bgf, wgb, bgb, sums_ref, level_ref, of_ref, ob_ref,
                sf_scr, sb_scr, a_scr, *, q_scale):
    @pl.when(pl.program_id(1) == 0)
    def _():
        sf_scr[...] = jnp.zeros_like(sf_scr)
        sb_scr[...] = jnp.zeros_like(sb_scr)

    streams = []
    for g in range(qf.shape[0]):
        streams.append(_GlaStream(qf.at[g], kf.at[g], vf.at[g], lrf.at[g], wgf, bgf, sums_ref.at[0], level_ref.at[0],
                                  of_ref.at[g], sf_scr.at[g], a_scr, False))
        streams.append(_GlaStream(qb.at[g], kb.at[g], vb.at[g], lrb.at[g], wgb, bgb, sums_ref.at[1], level_ref.at[1],
                                  ob_ref.at[g], sb_scr.at[g], a_scr, True))
    _gla_chunk(streams, q_scale)


def _gla_scans(u32, ubf, ulr, wg_f, bg_f, wg_b, bg_b, batch, t_pad, kw, vw, col_q, col_k, col_v):
    m = u32.shape[0]
    nc = t_pad // GLA_BLOCK
    dk = kw // GLA_HEADS
    dv = vw // GLA_HEADS
    group = 2 if batch % 2 == 0 else 1

    def per_batch(x):
        return x.reshape(batch, t_pad, x.shape[1])

    def fwd(col):
        return lambda b, c: (b, c, col)

    def bwd(col):
        return lambda b, c: (b, nc - 1 - c, col)

    def direction_specs(idx):
        return [pl.BlockSpec((group, GLA_BLOCK, kw), idx(col_q)),
                pl.BlockSpec((group, GLA_BLOCK, kw), idx(col_k)),
                pl.BlockSpec((group, GLA_BLOCK, vw), idx(col_v)),
                pl.BlockSpec((group, GLA_BLOCK, LANES), idx(0))]

    const = lambda b, c: (0, 0)
    gate_specs = [pl.BlockSpec((LANES, kw), const), pl.BlockSpec((1, kw), const),
                  pl.BlockSpec((LANES, kw), const), pl.BlockSpec((1, kw), const)]
    u32_3, ubf_3, ulr_3 = per_batch(u32), per_batch(ubf), per_batch(ulr)
    state = pltpu.VMEM((group, GLA_HEADS, dv, dk), F32)
    assert dk == LANES and GLA_HEADS % 2 == 0
    sums, pair_level = _gla_tables(GLA_BLOCK)
    table_specs = [pl.BlockSpec(sums.shape, lambda b, c: (0, 0, 0)),
                   pl.BlockSpec(pair_level.shape, lambda b, c: (0, 0, 0))]
    o_f, o_b = pl.pallas_call(
        functools.partial(_gla_kernel, q_scale=dk ** -0.5),
        grid=(batch // group, nc),
        in_specs=direction_specs(fwd) + direction_specs(bwd) + gate_specs + table_specs,
        out_specs=[pl.BlockSpec((group, GLA_BLOCK, vw), fwd(0)), pl.BlockSpec((group, GLA_BLOCK, vw), bwd(0))],
        out_shape=[jax.ShapeDtypeStruct((batch, t_pad, vw), F32), jax.ShapeDtypeStruct((batch, t_pad, vw), F32)],
        scratch_shapes=[state, state,
                        pltpu.VMEM((2 * group, GLA_HEADS // 2, GLA_BLOCK, 2 * GLA_BLOCK), F32)],
        compiler_params=_cparams("parallel", "arbitrary"),
        name="gla_scans",
    )(u32_3, u32_3, ubf_3, ulr_3, u32_3, u32_3, ubf_3, ulr_3, wg_f, bg_f.reshape(1, kw), wg_b, bg_b.reshape(1, kw),
      sums, pair_level)
    return o_f.reshape(m, vw), o_b.reshape(m, vw)


def _merge_kernel(xa_ref, xc_ref, of_ref, ob_ref, r_ref, ng_ref, wa_ref, wc_ref, wg_ref, bc_ref,
                  ga_ref, gc_ref, gg_ref, o_ref, xg_scr):
    dv = ng_ref.shape[1]
    for h in range(xg_scr.shape[1] // dv):
        vs = slice(h * dv, (h + 1) * dv)
        o = of_ref[:, vs] + ob_ref[:, vs]
        o = o * lax.rsqrt(jnp.mean(o * o, axis=-1, keepdims=True) + EPS) * ng_ref[...]
        r = r_ref[:, vs].astype(F32)
        xg_scr[:, vs] = (o * (r * jax.nn.sigmoid(r))).astype(xg_scr.dtype)
    ya = jnp.dot(xa_ref[...], wa_ref[...], preferred_element_type=F32)
    yc = jnp.dot(xc_ref[...], wc_ref[...], preferred_element_type=F32) + bc_ref[...]
    yg = jnp.dot(xg_scr[...], wg_ref[...], preferred_element_type=F32)
    merged = (jax.nn.sigmoid(ga_ref[...].astype(F32)) * ya + jax.nn.sigmoid(gc_ref[...].astype(F32)) * yc
              + jax.nn.sigmoid(gg_ref[...].astype(F32)) * yg)
    o_ref[...] = merged.astype(o_ref.dtype)


def _merge(x_da, x_cv, o_f, o_b, norm_g, w_da, w_cv, w_gla, layer, b_cv, ug, tm, gate_col0, col_r):
    m, kdim = x_da.shape
    d = w_da.shape[-1]
    dv = norm_g.shape[0]
    x_spec = pl.BlockSpec((tm, kdim), lambda i: (i, 0))
    w_spec = _weight_spec(w_da, layer, (kdim, d), lambda i: (0, 0), pipeline_mode=pl.Buffered(1))

    def gate_spec(branch):
        return pl.BlockSpec((tm, d), lambda i: (i, gate_col0 // d + branch))

    return pl.pallas_call(
        _merge_kernel,
        grid=(m // tm,),
        in_specs=[x_spec, x_spec, x_spec, x_spec,
                  pl.BlockSpec((tm, kdim), lambda i: (i, col_r)),
                  pl.BlockSpec((1, dv), lambda i: (0, 0)),
                  w_spec, w_spec, w_spec,
                  pl.BlockSpec((1, d), lambda i: (0, 0)),
                  gate_spec(0), gate_spec(1), gate_spec(2)],
        out_specs=pl.BlockSpec((tm, d), lambda i: (i, 0)),
        out_shape=jax.ShapeDtypeStruct((m, d), BF16),
        scratch_shapes=[pltpu.VMEM((tm, kdim), BF16)],
        compiler_params=_cparams("parallel"),
        name="branch_merge",
    )(x_da, x_cv, o_f, o_b, ug, norm_g.reshape(1, dv), w_da, w_cv, w_gla, b_cv.reshape(1, d), ug, ug, ug)


def _out_proj_kernel(x_ref, w_ref, h_ref, o_ref, *, tiles_per_batch):
    y = jnp.dot(x_ref[...], w_ref[...], preferred_element_type=F32)
    row = lax.broadcasted_iota(jnp.int32, y.shape, 0)
    first = pl.program_id(0) % tiles_per_batch == 0
    y = jnp.where(jnp.logical_and(first, row < PAD_FRONT), 0.0, y)
    o_ref[...] = h_ref[...] + y


def _out_proj(x, w, layer, h, tm, tiles_per_batch):
    m, kdim = x.shape
    d = w.shape[-1]
    return pl.pallas_call(
        functools.partial(_out_proj_kernel, tiles_per_batch=tiles_per_batch),
        grid=(m // tm,),
        in_specs=[pl.BlockSpec((tm, kdim), lambda i: (i, 0)),
                  _weight_spec(w, layer, (kdim, d), lambda i: (0, 0), pipeline_mode=pl.Buffered(1)),
                  pl.BlockSpec((tm, d), lambda i: (i, 0))],
        out_specs=pl.BlockSpec((tm, d), lambda i: (i, 0)),
        out_shape=jax.ShapeDtypeStruct((m, d), F32),
        compiler_params=_cparams("parallel"),
        name="out_proj",
    )(x, w, h)


def _mlp_kernel(h_ref, g_ref, w1_ref, w2_ref, gn_ref, *rest, final):
    if final:
        o_ref, a_scr = rest
    else:
        o_ref, an_ref, a_scr = rest

    @pl.when(pl.program_id(1) == 0)
    def _():
        x = h_ref[...]
        a_scr[...] = _rms(x, g_ref[...]).astype(a_scr.dtype)
        o_ref[...] = x

    t = jnp.maximum(jnp.dot(a_scr[...], w1_ref[...], preferred_element_type=F32), 0.0)
    o_ref[...] += jnp.dot((t * t).astype(BF16), w2_ref[...], preferred_element_type=F32)

    @pl.when(pl.program_id(1) == pl.num_programs(1) - 1)
    def _():
        y = _rms(o_ref[...], gn_ref[...])
        if final:
            o_ref[...] = y
        else:
            an_ref[...] = y.astype(an_ref.dtype)


def _mlp(h, g, w1, w2, layer, g_next, tm, tf, final, batch, seq):
    m, d = h.shape
    ff = w1.shape[-1]
    row_spec = pl.BlockSpec((tm, d), lambda i, f: (i, 0))
    vec_spec = pl.BlockSpec((1, d), lambda i, f: (0, 0))
    if final:
        t_pad, tiles = m // batch, seq // tm
        h_spec = pl.BlockSpec((pl.Element(tm), pl.Element(d)),
                              lambda i, f: (pl.multiple_of((i // tiles) * t_pad + (t_pad - seq) + (i % tiles) * tm,
                                                           SUBLANES), 0))
        n_rows, out_specs = batch * seq, row_spec
        out_shape = jax.ShapeDtypeStruct((n_rows, d), F32)
    else:
        h_spec, n_rows, out_specs = row_spec, m, [row_spec, row_spec]
        out_shape = [jax.ShapeDtypeStruct((m, d), F32), jax.ShapeDtypeStruct((m, d), BF16)]
    return pl.pallas_call(
        functools.partial(_mlp_kernel, final=final),
        grid=(n_rows // tm, ff // tf),
        in_specs=[h_spec, vec_spec,
                  _weight_spec(w1, layer, (d, tf), lambda i, f: (0, f)),
                  _weight_spec(w2, layer, (tf, d), lambda i, f: (f, 0)),
                  vec_spec],
        out_specs=out_specs,
        out_shape=out_shape,
        scratch_shapes=[pltpu.VMEM((tm, d), BF16)],
        compiler_params=_cparams("parallel", "arbitrary"),
        name="mlp",
    )(h, g.reshape(1, d), w1, w2, g_next.reshape(1, d))


def kernel(x, meta_tokens, mix_norm_g, w_in, da_lambda, da_subln_g, w_da_proj, conv_dw_w, conv_dw_b, conv_ln_g, conv_ln_b, w_conv_proj, b_conv_proj, gla_gate_w_fwd, gla_gate_b_fwd, gla_gate_w_bwd, gla_gate_b_bwd, gla_norm_g, w_gla_proj, w_out, mlp_norm_g, w_mlp_in, w_mlp_out, final_norm_g):
    batch, seq, d = x.shape
    depth = w_in.shape[0]
    assert seq % LANES == 0 and meta_tokens.shape[0] == N_META
    t_pad = PAD_FRONT + N_META + seq
    m = batch * t_pad

    da_hd = da_lambda.shape[-1]
    da_w = w_da_proj.shape[1]
    da_qk = DA_HEADS * 2 * da_hd
    conv_c = conv_dw_w.shape[-1]
    gla_kw = gla_gate_w_fwd.shape[-1]
    gla_vw = w_gla_proj.shape[1]
    d_ff = w_mlp_in.shape[-1]
    assert da_hd == LANES and da_w == DA_HEADS * 2 * da_hd

    sizes = (da_qk, da_qk, da_w, 2 * conv_c, gla_kw, gla_kw, gla_vw, 2 * GLA_GATE_RANK, gla_vw, N_BRANCH * d)
    assert sum(sizes) == w_in.shape[-1]
    off = [0]
    for s in sizes:
        off.append(off[-1] + s)
    c_dq, c_dk, c_dv, c_cu, c_gq, c_gk, c_gv, c_lr, c_gr, c_gt = off[:-1]

    tm = _row_tile(t_pad, 1152)
    tm_mlp = _row_tile(t_pad, 768)
    tm_mix = _row_tile(t_pad, 576)
    tm_merge = _row_tile(t_pad, 384)
    tiles_per_batch = t_pad // tm
    tq = _largest_divisor(t_pad, (384, 256, 128))
    tk_attn = _largest_divisor(seq, (1024, 512, 256, 128))
    tt_conv = _largest_divisor(t_pad, (384, 256, 128))
    tm_proj = 2 * tm if tiles_per_batch % 2 == 0 else tm
    tn_proj = 1024
    tf = 1024

    pos = (jnp.arange(t_pad, dtype=F32) - PAD_FRONT)[:, None]
    inv_freq = 1.0 / (ROPE_THETA ** (jnp.arange(0, da_hd, 2, dtype=F32) / da_hd))
    ang = pos * inv_freq[None, :]
    cos_t = jnp.concatenate([jnp.cos(ang), jnp.cos(ang)], axis=-1)
    sin_t = jnp.concatenate([-jnp.sin(ang), jnp.sin(ang)], axis=-1)

    h, a = _embed(x.astype(F32), meta_tokens.astype(F32), mix_norm_g[0])

    assert c_dq == 0 and c_dk == da_qk and da_w == gla_vw == tn_proj
    assert c_cu % tn_proj == 0 and (c_gq - c_cu) % gla_kw == 0 and c_gk + gla_kw - c_cu == 3 * tn_proj
    assert c_dv % tn_proj == 0 and c_gv % tn_proj == 0 and c_lr + 2 * GLA_GATE_RANK == c_gr
    tail_tiles = (gla_vw + N_BRANCH * d) // tn_proj

    w_in_bf, w_out_bf = w_in.astype(BF16), w_out.astype(BF16)
    w_da_bf, w_cv_bf, w_gla_bf = w_da_proj.astype(BF16), w_conv_proj.astype(BF16), w_gla_proj.astype(BF16)
    w_mlp_in_bf, w_mlp_out_bf = w_mlp_in.astype(BF16), w_mlp_out.astype(BF16)
    w_tail_bf = w_in_bf[:, :, c_gr:]
    w_lr_bf = jnp.concatenate([w_in_bf[:, :, c_lr:c_gr],
                               jnp.zeros((depth, d, LANES - 2 * GLA_GATE_RANK), BF16)], axis=2)
    tm_last = _largest_divisor(seq, (512, 256, 128))

    for l in range(depth):
        lam_init = 0.8 - 0.6 * math.exp(-0.3 * l)
        qk = _rope_proj(a, w_in_bf, l, 2 * da_qk, cos_t, sin_t, tm, tiles_per_batch,
                        da_hd ** -0.5 * math.log2(math.e))
        ulr = _proj(a, w_lr_bf, l, tm_proj, LANES, BF16, "proj_lr", 1, lambda j: 0)
        udv = _proj(a, w_in_bf, l, tm_proj, tn_proj, BF16, "proj_dv", 1, lambda j: c_dv // tn_proj)
        umid = _proj(a, w_in_bf, l, tm_proj, tn_proj, BF16, "proj_mid", (c_lr - c_cu) // tn_proj,
                     lambda j: c_cu // tn_proj + j)
        ug = _proj(a, w_tail_bf, l, tm_proj, tn_proj, BF16, "proj_gates", tail_tiles,
                   lambda j: (j + 1) % tail_tiles)
        m_gq, m_gk, m_gv = c_gq - c_cu, c_gk - c_cu, c_gv - c_cu

        vt = udv.reshape(batch, t_pad, da_w).transpose(0, 2, 1).reshape(batch * da_w, t_pad)
        x_da = _diff_attention(qk, vt, da_lambda[l].astype(F32), da_subln_g[l], batch, t_pad, tq, tk_attn, lam_init)
        x_cv = _conv_module(umid, conv_dw_w[l], conv_dw_b[l], conv_ln_g[l], conv_ln_b[l], batch, t_pad, tt_conv)

        gate_rows = jnp.zeros((LANES, gla_kw), F32)
        wg_f = gate_rows.at[:GLA_GATE_RANK].set(gla_gate_w_fwd[l]).astype(BF16)
        wg_b = gate_rows.at[GLA_GATE_RANK:2 * GLA_GATE_RANK].set(gla_gate_w_bwd[l]).astype(BF16)
        o_f, o_b = _gla_scans(umid, umid, ulr, wg_f, gla_gate_b_fwd[l], wg_b, gla_gate_b_bwd[l], batch, t_pad,
                              gla_kw, gla_vw, m_gq // gla_kw, m_gk // gla_kw, m_gv // gla_vw)

        merged = _merge(x_da, x_cv, o_f, o_b, gla_norm_g[l], w_da_bf, w_cv_bf, w_gla_bf, l, b_conv_proj[l], ug,
                        tm_merge, 0, N_BRANCH * d // gla_vw)
        h = _out_proj(merged, w_out_bf, l, h, tm_mix, t_pad // tm_mix)
        if l + 1 < depth:
            h, a = _mlp(h, mlp_norm_g[l], w_mlp_in_bf, w_mlp_out_bf, l, mix_norm_g[l + 1], tm_mlp, tf, False,
                        batch, seq)
        else:
            out = _mlp(h, mlp_norm_g[l], w_mlp_in_bf, w_mlp_out_bf, l, final_norm_g, tm_last, tf, True,
                       batch, seq)

    return out.reshape(batch, seq, d)
```

```python
import functools
import math

import jax
import jax.numpy as jnp
import numpy as np
from jax import lax
from jax.experimental import pallas as pl
from jax.experimental.pallas import tpu as pltpu

F32 = jnp.float32
BF16 = jnp.bfloat16

EPS = 1e-6
ROPE_THETA = 10000.0
N_META = 16
N_BRANCH = 3
DA_HEADS = 4
GLA_HEADS = 4
GLA_GATE_RANK = 16
GLA_TAU = 16.0
CONV_WIDTH = 31

LANES = 128
SUBLANES = 8
PAD_FRONT = LANES - N_META
GLA_BLOCK = 128
GLA_SAFE_LOG_DECAY = 80.0
CONV_HALO = 16
CONV_ROWS, CONV_COLS = 64, 256
VMEM_LIMIT = 56 * 1024 * 1024


def _cparams(*sem):
    return pltpu.CompilerParams(dimension_semantics=sem, vmem_limit_bytes=VMEM_LIMIT)


def _row_tile(t_pad, max_rows):
    for n in range(1, t_pad // 16 + 1):
        if t_pad % n == 0 and (t_pad // n) % 16 == 0 and t_pad // n <= max_rows:
            return t_pad // n
    raise ValueError(f"no row tile for padded sequence {t_pad}")


def _weight_spec(w, layer, block, index, **kwargs):
    if w.ndim == 2:
        return pl.BlockSpec(block, index, **kwargs)
    return pl.BlockSpec((None,) + tuple(block), lambda *g: (layer,) + tuple(index(*g)), **kwargs)


def _largest_divisor(n, candidates):
    for c in candidates:
        if n % c == 0:
            return c
    raise ValueError(f"no tile among {candidates} divides {n}")


def _rms(x, g):
    return x * lax.rsqrt(jnp.mean(x * x, axis=-1, keepdims=True) + EPS) * g


def _embed_kernel(*refs):
    *x_refs, meta_ref, g_ref, h_ref, a_ref = refs

    @pl.when(pl.program_id(1) == 0)
    def _():
        h_ref[0, 0:PAD_FRONT, :] = jnp.zeros((PAD_FRONT, h_ref.shape[2]), h_ref.dtype)
        h_ref[0, PAD_FRONT:LANES, :] = meta_ref[...]

    @pl.when(pl.program_id(1) > 0)
    def _():
        h_ref[0, 0:LANES, :] = x_refs[0][0]

    for j in range(1, len(x_refs)):
        h_ref[0, j * LANES:(j + 1) * LANES, :] = x_refs[j][0]
    a_ref[0] = _rms(h_ref[0], g_ref[...]).astype(a_ref.dtype)


def _embed(x, meta_tokens, g):
    batch, seq, d = x.shape
    blocks = (LANES + seq) // LANES
    per_step = _largest_divisor(blocks, (3, 2, 1))
    row_spec = pl.BlockSpec((1, per_step * LANES, d), lambda b, i: (b, i, 0))

    def x_spec(j):
        return pl.BlockSpec((1, LANES, d), lambda b, i: (b, jnp.maximum(i * per_step + j - 1, 0), 0))

    padded = (batch, blocks * LANES, d)
    h, a = pl.pallas_call(
        _embed_kernel,
        grid=(batch, blocks // per_step),
        in_specs=[x_spec(j) for j in range(per_step)] + [pl.BlockSpec((N_META, d), lambda b, i: (0, 0)),
                                                         pl.BlockSpec((1, d), lambda b, i: (0, 0))],
        out_specs=[row_spec, row_spec],
        out_shape=[jax.ShapeDtypeStruct(padded, F32), jax.ShapeDtypeStruct(padded, BF16)],
        compiler_params=_cparams("parallel", "arbitrary"),
        name="embed",
    )(*([x] * per_step), meta_tokens, g.reshape(1, d))
    return h.reshape(-1, d), a.reshape(-1, d)


def _proj_kernel(a_ref, w_ref, o_ref):
    o_ref[...] = jnp.dot(a_ref[...], w_ref[...], preferred_element_type=F32).astype(o_ref.dtype)


def _proj(a, w, layer, tm, tn, out_dtype, name, n_tiles, col_block):
    m, k = a.shape
    return pl.pallas_call(
        _proj_kernel,
        grid=(m // tm, n_tiles),
        in_specs=[pl.BlockSpec((tm, k), lambda i, j: (i, 0)),
                  _weight_spec(w, layer, (k, tn), lambda i, j: (0, col_block(j)))],
        out_specs=pl.BlockSpec((tm, tn), lambda i, j: (i, j)),
        out_shape=jax.ShapeDtypeStruct((m, n_tiles * tn), out_dtype),
        compiler_params=_cparams("parallel", "arbitrary"),
        name=name,
    )(a, w)


def _rope_proj_kernel(a_ref, w_ref, wlr_ref, cos_ref, sin_ref, o_ref, lr_ref, *, q_scale):
    lr_ref[...] = jnp.dot(a_ref[...], wlr_ref[...], preferred_element_type=F32).astype(lr_ref.dtype)
    acc = jnp.dot(a_ref[...], w_ref[...], preferred_element_type=F32)
    cos = cos_ref[...]
    sin = sin_ref[...]
    groups = acc.shape[1] // LANES
    for g in range(groups):
        x = acc[:, g * LANES:(g + 1) * LANES]
        swapped = pltpu.roll(x, LANES // 2, axis=1)
        y = x * cos + swapped * sin
        if g < groups // 2:
            y = y * q_scale
        o_ref[:, g * LANES:(g + 1) * LANES] = y.astype(o_ref.dtype)


def _rope_proj(a, w, w_lr, layer, n, cos, sin, tm, tiles_per_batch, q_scale):
    m, k = a.shape
    n_lr = w_lr.shape[-1]
    return pl.pallas_call(
        functools.partial(_rope_proj_kernel, q_scale=q_scale),
        grid=(m // tm,),
        in_specs=[pl.BlockSpec((tm, k), lambda i: (i, 0)),
                  _weight_spec(w, layer, (k, n), lambda i: (0, 0), pipeline_mode=pl.Buffered(1)),
                  _weight_spec(w_lr, layer, (k, n_lr), lambda i: (0, 0), pipeline_mode=pl.Buffered(1)),
                  pl.BlockSpec((tm, LANES), lambda i: (i % tiles_per_batch, 0)),
                  pl.BlockSpec((tm, LANES), lambda i: (i % tiles_per_batch, 0))],
        out_specs=[pl.BlockSpec((tm, n), lambda i: (i, 0)), pl.BlockSpec((tm, n_lr), lambda i: (i, 0))],
        out_shape=[jax.ShapeDtypeStruct((m, n), BF16), jax.ShapeDtypeStruct((m, n_lr), BF16)],
        compiler_params=_cparams("parallel"),
        name="rope_proj",
    )(a, w, w_lr, cos, sin)


def _attn_kernel(lam_ref, q_ref, k_ref, vt_ref, g_ref, o_ref, s_scr, acc_scr, *, lam_init, tk):
    tq = q_ref.shape[0]
    d = q_ref.shape[1] // 2
    t_pad = k_ref.shape[0]
    chunks = [(0, LANES)] + [(r, tk) for r in range(LANES, t_pad, tk)]

    def fold8(x):
        return x.reshape(x.shape[0] // 8, 8, tq)

    row_ok = lax.broadcasted_iota(jnp.int32, (LANES, tq), 0) >= PAD_FRONT
    col_max = []
    for m in range(2):
        q = q_ref[:, m * d:(m + 1) * d]
        part = None
        for r0, n in chunks:
            s = lax.dot_general(k_ref[r0:r0 + n, m * d:(m + 1) * d], q, (((1,), (1,)), ((), ())),
                                preferred_element_type=F32)
            if r0 == 0:
                s = jnp.where(row_ok, s, -1e30)
            s_scr[m, r0:r0 + n, :] = s
            cmax = jnp.max(fold8(s), axis=0)
            part = cmax if part is None else jnp.maximum(part, cmax)
        col_max.append(jnp.max(part, axis=0, keepdims=True))

    col_sum = []
    for m in range(2):
        part = None
        for r0, n in chunks:
            e = jnp.exp2(s_scr[m, r0:r0 + n, :] - col_max[m])
            csum = jnp.sum(fold8(e), axis=0)
            part = csum if part is None else part + csum
            pv = jnp.dot(vt_ref[:, r0:r0 + n], e.astype(BF16), preferred_element_type=F32)
            if r0 == 0:
                acc_scr[m] = pv
            else:
                acc_scr[m] += pv
        col_sum.append(jnp.sum(part, axis=0, keepdims=True))

    lam4 = lam_ref[...]
    dot1 = jnp.sum(lam4[0:1] * lam4[1:2], axis=-1, keepdims=True)
    dot2 = jnp.sum(lam4[2:3] * lam4[3:4], axis=-1, keepdims=True)
    lam = jnp.exp(dot1) - jnp.exp(dot2) + lam_init
    ot = acc_scr[0] * (1.0 / col_sum[0]) - acc_scr[1] * (lam / col_sum[1])
    o = ot.T
    o = o * lax.rsqrt(jnp.mean(o * o, axis=-1, keepdims=True) + EPS) * g_ref[...]
    o_ref[...] = (o * (1.0 - lam_init)).astype(o_ref.dtype)


def _diff_attention(qk, vt, lam4, subln_g, batch, t_pad, tq, tk, lam_init):
    m = qk.shape[0]
    hw = 2 * lam4.shape[1]
    nq = t_pad // tq
    return pl.pallas_call(
        functools.partial(_attn_kernel, lam_init=lam_init, tk=tk),
        grid=(batch, DA_HEADS, nq),
        in_specs=[pl.BlockSpec(lam4.shape, lambda b, h, i: (0, 0)),
                  pl.BlockSpec((tq, hw), lambda b, h, i: (b * nq + i, h)),
                  pl.BlockSpec((t_pad, hw), lambda b, h, i: (b, DA_HEADS + h)),
                  pl.BlockSpec((hw, t_pad), lambda b, h, i: (b * DA_HEADS + h, 0)),
                  pl.BlockSpec((1, hw), lambda b, h, i: (0, 0))],
        out_specs=pl.BlockSpec((tq, hw), lambda b, h, i: (b * nq + i, h)),
        out_shape=jax.ShapeDtypeStruct((m, DA_HEADS * hw), BF16),
        scratch_shapes=[pltpu.VMEM((2, t_pad, tq), F32), pltpu.VMEM((2, hw, tq), F32)],
        compiler_params=_cparams("parallel", "parallel", "arbitrary"),
        name="diff_attention",
    )(lam4, qk, qk, vt, subln_g.reshape(1, hw))


def _conv_kernel(prev_ref, cur_ref, next_ref, w_ref, b_ref, lg_ref, lb_ref, o_ref, z_scr, zs_scr, y_scr,
                 *, n_tiles):
    i = pl.program_id(1)
    c = o_ref.shape[1]
    tt = o_ref.shape[0]

    def glu(u):
        u = u.astype(F32)
        return u[:, :c] * jax.nn.sigmoid(u[:, c:])

    z_scr[0:CONV_HALO, :] = jnp.where(i > 0, glu(prev_ref[...]), 0.0)
    z_scr[CONV_HALO:CONV_HALO + tt, :] = glu(cur_ref[...])
    z_scr[CONV_HALO + tt:, :] = jnp.where(i < n_tiles - 1, glu(next_ref[...]), 0.0)
    n_shift = zs_scr.shape[1]
    for s in range(1, SUBLANES):
        zs_scr[s - 1] = z_scr[s:s + n_shift, :]
    half = CONV_WIDTH // 2
    rows = _largest_divisor(tt, (CONV_ROWS, SUBLANES))
    for r in range(0, tt, rows):
        for c0 in range(0, c, CONV_COLS):
            acc = b_ref[:, c0:c0 + CONV_COLS]
            for k in range(CONV_WIDTH):
                base, s = divmod(CONV_HALO - half + k, SUBLANES)
                src = z_scr if s == 0 else zs_scr.at[s - 1]
                r_src = base * SUBLANES + r
                acc = acc + src[r_src:r_src + rows, c0:c0 + CONV_COLS] * w_ref[k:k + 1, c0:c0 + CONV_COLS]
            y_scr[r:r + rows, c0:c0 + CONV_COLS] = acc
    for r in range(0, tt, rows):
        y = y_scr[r:r + rows, :]
        mu = jnp.mean(y, axis=-1, keepdims=True)
        yc = y - mu
        var = jnp.mean(yc * yc, axis=-1, keepdims=True)
        y = yc * lax.rsqrt(var + EPS) * lg_ref[...] + lb_ref[...]
        o_ref[r:r + rows, :] = (y * jax.nn.sigmoid(y)).astype(o_ref.dtype)


def _conv_module(u32, dw_w, dw_b, ln_g, ln_b, batch, t_pad, tt):
    m = u32.shape[0]
    c = dw_w.shape[1]
    nt = t_pad // tt
    hb = tt // CONV_HALO
    last_halo = m // CONV_HALO - 1
    return pl.pallas_call(
        functools.partial(_conv_kernel, n_tiles=nt),
        grid=(batch, nt),
        in_specs=[pl.BlockSpec((CONV_HALO, 2 * c), lambda b, i: (jnp.maximum((b * nt + i) * hb - 1, 0), 0)),
                  pl.BlockSpec((tt, 2 * c), lambda b, i: (b * nt + i, 0)),
                  pl.BlockSpec((CONV_HALO, 2 * c), lambda b, i: (jnp.minimum((b * nt + i + 1) * hb, last_halo), 0)),
                  pl.BlockSpec((CONV_WIDTH, c), lambda b, i: (0, 0)),
                  pl.BlockSpec((1, c), lambda b, i: (0, 0)),
                  pl.BlockSpec((1, c), lambda b, i: (0, 0)),
                  pl.BlockSpec((1, c), lambda b, i: (0, 0))],
        out_specs=pl.BlockSpec((tt, c), lambda b, i: (b * nt + i, 0)),
        out_shape=jax.ShapeDtypeStruct((m, c), BF16),
        scratch_shapes=[pltpu.VMEM((tt + 2 * CONV_HALO, c), F32),
                        pltpu.VMEM((SUBLANES - 1, tt + 2 * CONV_HALO - SUBLANES, c), F32),
                        pltpu.VMEM((tt, c), F32)],
        compiler_params=_cparams("parallel", "arbitrary"),
        name="conv_module",
    )(u32, u32, u32, dw_w, dw_b.reshape(1, c), ln_g.reshape(1, c), ln_b.reshape(1, c))


def _log_sigmoid(x):
    return jnp.minimum(x, 0.0) - jnp.log(1.0 + jnp.exp(-jnp.abs(x)))


def _gla_tables(n):
    levels = n.bit_length() - 1
    assert n == 1 << levels
    sums, pair_level = [], []
    for reverse in (False, True):
        pos = np.arange(n)[::-1] if reverse else np.arange(n)
        incl = (pos[None, :] <= pos[:, None]).astype(np.float32)
        blocks = [incl]
        for k in reversed(range(levels)):
            half = 1 << k
            boundary = (pos // (2 * half)) * (2 * half) + half - 1
            blocks.append(incl - (pos[None, :] <= boundary[:, None]).astype(np.float32))
        stacked = np.concatenate(blocks, axis=0)
        sums.append(np.concatenate([stacked, stacked], axis=1))
        differ = pos[:, None] ^ pos[None, :]
        level = np.floor(np.log2(np.maximum(differ, 1))).astype(np.int32) + 1
        level = np.where(pos[:, None] > pos[None, :], level, np.where(differ == 0, 0, -1))
        pair_level.append(np.concatenate([level, level], axis=1))
    return jnp.asarray(np.stack(sums), BF16), jnp.asarray(np.stack(pair_level), jnp.int32)


class _GlaStream:
    def __init__(self, q_ref, k_ref, v_ref, lr_ref, wg_ref, bg_ref, sums_ref, level_ref, o_ref, s_scr, a_scr,
                 reverse):
        self.q_ref, self.k_ref, self.v_ref, self.lr_ref = q_ref, k_ref, v_ref, lr_ref
        self.wg_ref, self.bg_ref, self.o_ref, self.s_scr = wg_ref, bg_ref, o_ref, s_scr
        self.sums_ref, self.level_ref, self.a_scr = sums_ref, level_ref, a_scr
        self.edge = 0 if reverse else q_ref.shape[0] - 1


def _gla_chunk(streams, q_scale):
    heads, dv, dk = streams[0].s_scr.shape
    n = streams[0].q_ref.shape[0]
    levels = n.bit_length() - 1
    nt = (((1,), (1,)), ((), ()))
    first_head = lax.broadcasted_iota(jnp.int32, (n, 2 * dk), 1) < dk
    for s in streams:
        gate = jnp.dot(s.lr_ref[...], s.wg_ref[...], preferred_element_type=F32) + s.bg_ref[...]
        la = _log_sigmoid(gate) * (1.0 / GLA_TAU)
        la_hi = la.astype(BF16)
        s.la_split = jnp.concatenate([la_hi, (la - la_hi.astype(F32)).astype(BF16)], axis=0)
    for s in streams:
        s.b = jnp.dot(s.sums_ref[0:n, :], s.la_split, preferred_element_type=F32)
    worst = None
    for s in streams:
        b = s.b
        b_edge = b[s.edge:s.edge + 1, :]
        s.q = s.q_ref[...].astype(F32) * q_scale
        s.k = s.k_ref[...].astype(F32)
        s.q_inter = (s.q * jnp.exp(b)).astype(BF16)
        s.k_state = (s.k * jnp.exp(b_edge - b)).astype(BF16)
        s.decay = jnp.exp(b_edge)
        total = jnp.max(-b_edge)
        worst = total if worst is None else jnp.maximum(worst, total)

    def pair_scores(s, ql, kl, pair):
        cs = slice(2 * pair * dk, 2 * (pair + 1) * dk)
        kp = kl[:, cs]
        k_blockdiag = jnp.concatenate([jnp.where(first_head, kp, 0), jnp.where(first_head, 0, kp)], axis=0)
        return lax.dot_general(ql[:, cs], k_blockdiag, nt, preferred_element_type=F32)

    @pl.when(worst <= GLA_SAFE_LOG_DECAY)
    def _():
        for i, s in enumerate(streams):
            mid = n // 2 if s.edge == 0 else n // 2 - 1
            b_mid = s.b[mid:mid + 1, :]
            ql = (s.q * jnp.exp(s.b - b_mid)).astype(BF16)
            kl = (s.k * jnp.exp(b_mid - s.b)).astype(BF16)
            level = s.level_ref[...]
            for pair in range(heads // 2):
                s.a_scr[i, pair] = jnp.where(level >= 0, pair_scores(s, ql, kl, pair), 0.0)

    @pl.when(worst > GLA_SAFE_LOG_DECAY)
    def _():
        for i, s in enumerate(streams):
            sums = jnp.dot(s.sums_ref[n:, :], s.la_split, preferred_element_type=F32)
            level = s.level_ref[...]
            acc = [jnp.zeros((n, 2 * n), F32) for _ in range(heads // 2)]
            for code in range(levels + 1):
                if code == 0:
                    ql, kl = s.q.astype(BF16), s.k.astype(BF16)
                else:
                    block = levels - code
                    f = jnp.exp(-jnp.abs(sums[block * n:(block + 1) * n]))
                    ql, kl = (s.q * f).astype(BF16), (s.k * f).astype(BF16)
                for pair in range(heads // 2):
                    acc[pair] = acc[pair] + jnp.where(level == code, pair_scores(s, ql, kl, pair), 0.0)
            for pair in range(heads // 2):
                s.a_scr[i, pair] = acc[pair]

    for s in streams:
        s.o_inter = [lax.dot_general(s.q_inter[:, h * dk:(h + 1) * dk], s.s_scr[h].astype(BF16), nt,
                                     preferred_element_type=F32) for h in range(heads)]
    for i, s in enumerate(streams):
        for h in range(heads):
            vs = slice(h * dv, (h + 1) * dv)
            a = s.a_scr[i, h // 2, :, (h % 2) * n:(h % 2 + 1) * n].astype(BF16)
            s.o_ref[:, vs] = s.o_inter[h] + jnp.dot(a, s.v_ref[:, vs], preferred_element_type=F32)
    for s in streams:
        for h in range(heads):
            cs = slice(h * dk, (h + 1) * dk)
            ds = lax.dot_general(s.v_ref[:, h * dv:(h + 1) * dv], s.k_state[:, cs], (((0,), (0,)), ((), ())),
                                 preferred_element_type=F32)
            s.s_scr[h] = s.s_scr[h] * s.decay[:, cs] + ds


def _gla_kernel(qf, kf, vf, lrf, qb, kb, vb, lrb, wgf, bgf, wgb, bgb, sums_ref, level_ref, of_ref, ob_ref,
                sf_scr, sb_scr, a_scr, *, q_scale):
    @pl.when(pl.program_id(1) == 0)
    def _():
        sf_scr[...] = jnp.zeros_like(sf_scr)
        sb_scr[...] = jnp.zeros_like(sb_scr)

    streams = []
    for g in range(qf.shape[0]):
        streams.append(_GlaStream(qf.at[g], kf.at[g], vf.at[g], lrf.at[g], wgf, bgf, sums_ref.at[0], level_ref.at[0],
                                  of_ref.at[g], sf_scr.at[g], a_scr, False))
        streams.append(_GlaStream(qb.at[g], kb.at[g], vb.at[g], lrb.at[g], wgb, bgb, sums_ref.at[1], level_ref.at[1],
                                  ob_ref.at[g], sb_scr.at[g], a_scr, True))
    _gla_chunk(streams, q_scale)


def _gla_scans(u32, ubf, ulr, wg_f, bg_f, wg_b, bg_b, batch, t_pad, kw, vw, col_q, col_k, col_v):
    m = u32.shape[0]
    nc = t_pad // GLA_BLOCK
    dk = kw // GLA_HEADS
    dv = vw // GLA_HEADS
    group = _largest_divisor(batch, (4, 2, 1))

    def per_batch(x):
        return x.reshape(batch, t_pad, x.shape[1])

    def fwd(col):
        return lambda b, c: (b, c, col)

    def bwd(col):
        return lambda b, c: (b, nc - 1 - c, col)

    def direction_specs(idx):
        return [pl.BlockSpec((group, GLA_BLOCK, kw), idx(col_q)),
                pl.BlockSpec((group, GLA_BLOCK, kw), idx(col_k)),
                pl.BlockSpec((group, GLA_BLOCK, vw), idx(col_v)),
                pl.BlockSpec((group, GLA_BLOCK, LANES), idx(0))]

    const = lambda b, c: (0, 0)
    gate_specs = [pl.BlockSpec((LANES, kw), const), pl.BlockSpec((1, kw), const),
                  pl.BlockSpec((LANES, kw), const), pl.BlockSpec((1, kw), const)]
    u32_3, ubf_3, ulr_3 = per_batch(u32), per_batch(ubf), per_batch(ulr)
    state = pltpu.VMEM((group, GLA_HEADS, dv, dk), F32)
    assert dk == LANES and GLA_HEADS % 2 == 0
    sums, pair_level = _gla_tables(GLA_BLOCK)
    table_specs = [pl.BlockSpec(sums.shape, lambda b, c: (0, 0, 0)),
                   pl.BlockSpec(pair_level.shape, lambda b, c: (0, 0, 0))]
    o_f, o_b = pl.pallas_call(
        functools.partial(_gla_kernel, q_scale=dk ** -0.5),
        grid=(batch // group, nc),
        in_specs=direction_specs(fwd) + direction_specs(bwd) + gate_specs + table_specs,
        out_specs=[pl.BlockSpec((group, GLA_BLOCK, vw), fwd(0)), pl.BlockSpec((group, GLA_BLOCK, vw), bwd(0))],
        out_shape=[jax.ShapeDtypeStruct((batch, t_pad, vw), F32), jax.ShapeDtypeStruct((batch, t_pad, vw), F32)],
        scratch_shapes=[state, state,
                        pltpu.VMEM((2 * group, GLA_HEADS // 2, GLA_BLOCK, 2 * GLA_BLOCK), F32)],
        compiler_params=_cparams("parallel", "arbitrary"),
        name="gla_scans",
    )(u32_3, u32_3, ubf_3, ulr_3, u32_3, u32_3, ubf_3, ulr_3, wg_f, bg_f.reshape(1, kw), wg_b, bg_b.reshape(1, kw),
      sums, pair_level)
    return o_f.reshape(m, vw), o_b.reshape(m, vw)


def _merge_kernel(xa_ref, xc_ref, of_ref, ob_ref, r_ref, ng_ref, wa_ref, wc_ref, wg_ref, bc_ref,
                  ga_ref, gc_ref, gg_ref, o_ref, xg_scr):
    dv = ng_ref.shape[1]
    for h in range(xg_scr.shape[1] // dv):
        vs = slice(h * dv, (h + 1) * dv)
        o = of_ref[:, vs] + ob_ref[:, vs]
        o = o * lax.rsqrt(jnp.mean(o * o, axis=-1, keepdims=True) + EPS) * ng_ref[...]
        r = r_ref[:, vs].astype(F32)
        xg_scr[:, vs] = (o * (r * jax.nn.sigmoid(r))).astype(xg_scr.dtype)
    ya = jnp.dot(xa_ref[...], wa_ref[...], preferred_element_type=F32)
    yc = jnp.dot(xc_ref[...], wc_ref[...], preferred_element_type=F32) + bc_ref[...]
    yg = jnp.dot(xg_scr[...], wg_ref[...], preferred_element_type=F32)
    merged = (jax.nn.sigmoid(ga_ref[...].astype(F32)) * ya + jax.nn.sigmoid(gc_ref[...].astype(F32)) * yc
              + jax.nn.sigmoid(gg_ref[...].astype(F32)) * yg)
    o_ref[...] = merged.astype(o_ref.dtype)


def _merge(x_da, x_cv, o_f, o_b, norm_g, w_da, w_cv, w_gla, layer, b_cv, ug, tm, gate_col0, col_r):
    m, kdim = x_da.shape
    d = w_da.shape[-1]
    dv = norm_g.shape[0]
    x_spec = pl.BlockSpec((tm, kdim), lambda i: (i, 0))
    w_spec = _weight_spec(w_da, layer, (kdim, d), lambda i: (0, 0), pipeline_mode=pl.Buffered(1))

    def gate_spec(branch):
        return pl.BlockSpec((tm, d), lambda i: (i, gate_col0 // d + branch))

    return pl.pallas_call(
        _merge_kernel,
        grid=(m // tm,),
        in_specs=[x_spec, x_spec, x_spec, x_spec,
                  pl.BlockSpec((tm, kdim), lambda i: (i, col_r)),
                  pl.BlockSpec((1, dv), lambda i: (0, 0)),
                  w_spec, w_spec, w_spec,
                  pl.BlockSpec((1, d), lambda i: (0, 0)),
                  gate_spec(0), gate_spec(1), gate_spec(2)],
        out_specs=pl.BlockSpec((tm, d), lambda i: (i, 0)),
        out_shape=jax.ShapeDtypeStruct((m, d), BF16),
        scratch_shapes=[pltpu.VMEM((tm, kdim), BF16)],
        compiler_params=_cparams("parallel"),
        name="branch_merge",
    )(x_da, x_cv, o_f, o_b, ug, norm_g.reshape(1, dv), w_da, w_cv, w_gla, b_cv.reshape(1, d), ug, ug, ug)


def _out_proj_kernel(x_ref, w_ref, h_ref, o_ref, *, tiles_per_batch):
    y = jnp.dot(x_ref[...], w_ref[...], preferred_element_type=F32)
    row = lax.broadcasted_iota(jnp.int32, y.shape, 0)
    first = pl.program_id(0) % tiles_per_batch == 0
    y = jnp.where(jnp.logical_and(first, row < PAD_FRONT), 0.0, y)
    o_ref[...] = h_ref[...] + y


def _out_proj(x, w, layer, h, tm, tiles_per_batch):
    m, kdim = x.shape
    d = w.shape[-1]
    return pl.pallas_call(
        functools.partial(_out_proj_kernel, tiles_per_batch=tiles_per_batch),
        grid=(m // tm,),
        in_specs=[pl.BlockSpec((tm, kdim), lambda i: (i, 0)),
                  _weight_spec(w, layer, (kdim, d), lambda i: (0, 0), pipeline_mode=pl.Buffered(1)),
                  pl.BlockSpec((tm, d), lambda i: (i, 0))],
        out_specs=pl.BlockSpec((tm, d), lambda i: (i, 0)),
        out_shape=jax.ShapeDtypeStruct((m, d), F32),
        compiler_params=_cparams("parallel"),
        name="out_proj",
    )(x, w, h)


def _mlp_kernel(h_ref, g_ref, w1_ref, w2_ref, gn_ref, *rest, final):
    if final:
        o_ref, a_scr = rest
    else:
        o_ref, an_ref, a_scr = rest

    @pl.when(pl.program_id(1) == 0)
    def _():
        x = h_ref[...]
        a_scr[...] = _rms(x, g_ref[...]).astype(a_scr.dtype)
        o_ref[...] = x

    t = jnp.maximum(jnp.dot(a_scr[...], w1_ref[...], preferred_element_type=F32), 0.0)
    o_ref[...] += jnp.dot((t * t).astype(BF16), w2_ref[...], preferred_element_type=F32)

    @pl.when(pl.program_id(1) == pl.num_programs(1) - 1)
    def _():
        y = _rms(o_ref[...], gn_ref[...])
        if final:
            o_ref[...] = y
        else:
            an_ref[...] = y.astype(an_ref.dtype)


def _mlp(h, g, w1, w2, layer, g_next, tm, tf, final, batch, seq):
    m, d = h.shape
    ff = w1.shape[-1]
    row_spec = pl.BlockSpec((tm, d), lambda i, f: (i, 0))
    vec_spec = pl.BlockSpec((1, d), lambda i, f: (0, 0))
    if final:
        t_pad, tiles = m // batch, seq // tm
        h_spec = pl.BlockSpec((pl.Element(tm), pl.Element(d)),
                              lambda i, f: (pl.multiple_of((i // tiles) * t_pad + (t_pad - seq) + (i % tiles) * tm,
                                                           SUBLANES), 0))
        n_rows, out_specs = batch * seq, row_spec
        out_shape = jax.ShapeDtypeStruct((n_rows, d), F32)
    else:
        h_spec, n_rows, out_specs = row_spec, m, [row_spec, row_spec]
        out_shape = [jax.ShapeDtypeStruct((m, d), F32), jax.ShapeDtypeStruct((m, d), BF16)]
    return pl.pallas_call(
        functools.partial(_mlp_kernel, final=final),
        grid=(n_rows // tm, ff // tf),
        in_specs=[h_spec, vec_spec,
                  _weight_spec(w1, layer, (d, tf), lambda i, f: (0, f)),
                  _weight_spec(w2, layer, (tf, d), lambda i, f: (f, 0)),
                  vec_spec],
        out_specs=out_specs,
        out_shape=out_shape,
        scratch_shapes=[pltpu.VMEM((tm, d), BF16)],
        compiler_params=_cparams("parallel", "arbitrary"),
        name="mlp",
    )(h, g.reshape(1, d), w1, w2, g_next.reshape(1, d))


def kernel(x, meta_tokens, mix_norm_g, w_in, da_lambda, da_subln_g, w_da_proj, conv_dw_w, conv_dw_b, conv_ln_g, conv_ln_b, w_conv_proj, b_conv_proj, gla_gate_w_fwd, gla_gate_b_fwd, gla_gate_w_bwd, gla_gate_b_bwd, gla_norm_g, w_gla_proj, w_out, mlp_norm_g, w_mlp_in, w_mlp_out, final_norm_g):
    batch, seq, d = x.shape
    depth = w_in.shape[0]
    assert seq % LANES == 0 and meta_tokens.shape[0] == N_META
    t_pad = PAD_FRONT + N_META + seq
    m = batch * t_pad

    da_hd = da_lambda.shape[-1]
    da_w = w_da_proj.shape[1]
    da_qk = DA_HEADS * 2 * da_hd
    conv_c = conv_dw_w.shape[-1]
    gla_kw = gla_gate_w_fwd.shape[-1]
    gla_vw = w_gla_proj.shape[1]
    d_ff = w_mlp_in.shape[-1]
    assert da_hd == LANES and da_w == DA_HEADS * 2 * da_hd

    sizes = (da_qk, da_qk, da_w, 2 * conv_c, gla_kw, gla_kw, gla_vw, 2 * GLA_GATE_RANK, gla_vw, N_BRANCH * d)
    assert sum(sizes) == w_in.shape[-1]
    off = [0]
    for s in sizes:
        off.append(off[-1] + s)
    c_dq, c_dk, c_dv, c_cu, c_gq, c_gk, c_gv, c_lr, c_gr, c_gt = off[:-1]

    tm = _row_tile(t_pad, 1152)
    tm_mlp = _row_tile(t_pad, 768)
    tm_mix = _row_tile(t_pad, 576)
    tm_merge = _row_tile(t_pad, 384)
    tiles_per_batch = t_pad // tm
    tq = _largest_divisor(t_pad, (384, 256, 128))
    tk_attn = _largest_divisor(seq, (1024, 512, 256, 128))
    tt_conv = _largest_divisor(t_pad, (384, 256, 128))
    tm_proj = 2 * tm if tiles_per_batch % 2 == 0 else tm
    tn_proj = 1024
    tf = 1024

    pos = (jnp.arange(t_pad, dtype=F32) - PAD_FRONT)[:, None]
    inv_freq = 1.0 / (ROPE_THETA ** (jnp.arange(0, da_hd, 2, dtype=F32) / da_hd))
    ang = pos * inv_freq[None, :]
    cos_t = jnp.concatenate([jnp.cos(ang), jnp.cos(ang)], axis=-1)
    sin_t = jnp.concatenate([-jnp.sin(ang), jnp.sin(ang)], axis=-1)

    h, a = _embed(x.astype(F32), meta_tokens.astype(F32), mix_norm_g[0])

    assert c_dq == 0 and c_dk == da_qk and da_w == gla_vw == tn_proj
    assert c_cu % tn_proj == 0 and (c_gq - c_cu) % gla_kw == 0 and c_gk + gla_kw - c_cu == 3 * tn_proj
    assert c_dv % tn_proj == 0 and c_gv % tn_proj == 0 and c_lr + 2 * GLA_GATE_RANK == c_gr
    tail_tiles = (gla_vw + N_BRANCH * d) // tn_proj

    w_in_bf, w_out_bf = w_in.astype(BF16), w_out.astype(BF16)
    w_da_bf, w_cv_bf, w_gla_bf = w_da_proj.astype(BF16), w_conv_proj.astype(BF16), w_gla_proj.astype(BF16)
    w_mlp_in_bf, w_mlp_out_bf = w_mlp_in.astype(BF16), w_mlp_out.astype(BF16)
    w_tail_bf = w_in_bf[:, :, c_gr:]
    w_lr_bf = jnp.concatenate([w_in_bf[:, :, c_lr:c_gr],
                               jnp.zeros((depth, d, LANES - 2 * GLA_GATE_RANK), BF16)], axis=2)
    tm_last = _largest_divisor(seq, (512, 256, 128))

    for l in range(depth):
        lam_init = 0.8 - 0.6 * math.exp(-0.3 * l)
        qk, ulr = _rope_proj(a, w_in_bf, w_lr_bf, l, 2 * da_qk, cos_t, sin_t, tm, tiles_per_batch,
                             da_hd ** -0.5 * math.log2(math.e))
        udv = _proj(a, w_in_bf, l, tm_proj, tn_proj, BF16, "proj_dv", 1, lambda j: c_dv // tn_proj)
        umid = _proj(a, w_in_bf, l, tm_proj, tn_proj, BF16, "proj_mid", (c_lr - c_cu) // tn_proj,
                     lambda j: c_cu // tn_proj + j)
        ug = _proj(a, w_tail_bf, l, tm_proj, tn_proj, BF16, "proj_gates", tail_tiles,
                   lambda j: (j + 1) % tail_tiles)
        m_gq, m_gk, m_gv = c_gq - c_cu, c_gk - c_cu, c_gv - c_cu

        vt = udv.reshape(batch, t_pad, da_w).transpose(0, 2, 1).reshape(batch * da_w, t_pad)
        x_da = _diff_attention(qk, vt, da_lambda[l].astype(F32), da_subln_g[l], batch, t_pad, tq, tk_attn, lam_init)
        x_cv = _conv_module(umid, conv_dw_w[l], conv_dw_b[l], conv_ln_g[l], conv_ln_b[l], batch, t_pad, tt_conv)

        gate_rows = jnp.zeros((LANES, gla_kw), F32)
        wg_f = gate_rows.at[:GLA_GATE_RANK].set(gla_gate_w_fwd[l]).astype(BF16)
        wg_b = gate_rows.at[GLA_GATE_RANK:2 * GLA_GATE_RANK].set(gla_gate_w_bwd[l]).astype(BF16)
        o_f, o_b = _gla_scans(umid, umid, ulr, wg_f, gla_gate_b_fwd[l], wg_b, gla_gate_b_bwd[l], batch, t_pad,
                              gla_kw, gla_vw, m_gq // gla_kw, m_gk // gla_kw, m_gv // gla_vw)

        merged = _merge(x_da, x_cv, o_f, o_b, gla_norm_g[l], w_da_bf, w_cv_bf, w_gla_bf, l, b_conv_proj[l], ug,
                        tm_merge, 0, N_BRANCH * d // gla_vw)
        h = _out_proj(merged, w_out_bf, l, h, tm_mix, t_pad // tm_mix)
        if l + 1 < depth:
            h, a = _mlp(h, mlp_norm_g[l], w_mlp_in_bf, w_mlp_out_bf, l, mix_norm_g[l + 1], tm_mlp, tf, False,
                        batch, seq)
        else:
            out = _mlp(h, mlp_norm_g[l], w_mlp_in_bf, w_mlp_out_bf, l, final_norm_g, tm_last, tf, True,
                       batch, seq)

    return out.reshape(batch, seq, d)
```
